```python
import math
import jax, jax.numpy as jnp
from jax import lax
import numpy as np

D_MODEL = 1024
BATCH = 2
SEQ = 16384
DEPTH = 2

FOX_HEADS = 8
FOX_HEAD_DIM = 64
DIFF_HEADS = 4
DIFF_HEAD_DIM = 64
DIFF_V_DIM = 2 * DIFF_HEAD_DIM
FOX_WIDTH = FOX_HEADS * FOX_HEAD_DIM
DIFF_WIDTH = DIFF_HEADS * DIFF_V_DIM
MIX_WIDTH = FOX_WIDTH + DIFF_WIDTH
D_FF = 2816
Q_BLOCK = 128
ROPE_THETA = 10000.0
NORM_EPS = 1e-5
FFN_RES_WEIGHT = 0.5
IN_SPLITS = [FOX_WIDTH, FOX_WIDTH, FOX_WIDTH, FOX_HEADS,
             2 * DIFF_HEADS * DIFF_HEAD_DIM, 2 * DIFF_HEADS * DIFF_HEAD_DIM, DIFF_WIDTH]
IN_COLS = int(sum(IN_SPLITS))
IN_OFFSETS = [int(v) for v in np.cumsum(IN_SPLITS)[:-1]]

kernel_name = "hymba_fox_diffattn_macaron"


def _rmsnorm(x, gain):
    x32 = x.astype(jnp.float32)
    y = x32 * lax.rsqrt(jnp.mean(x32 * x32, axis=-1, keepdims=True) + NORM_EPS)
    return (y * gain.astype(jnp.float32)).astype(x.dtype)


def _swiglu(h, w_gate_up, w_down):
    g, u = jnp.split(h @ w_gate_up, 2, axis=-1)
    return (jax.nn.silu(g) * u) @ w_down


def _rope_tables(seq, dim):
    inv_freq = 1.0 / (ROPE_THETA ** (jnp.arange(0, dim, 2, dtype=jnp.float32) / dim))
    pos = jnp.arange(seq, dtype=jnp.float32)
    freqs = pos[:, None] * inv_freq[None, :]
    emb = jnp.concatenate([freqs, freqs], axis=-1)
    return jnp.cos(emb), jnp.sin(emb)


def _apply_rope(t, cos, sin):
    t32 = t.astype(jnp.float32)
    t1, t2 = jnp.split(t32, 2, axis=-1)
    rot = jnp.concatenate([-t2, t1], axis=-1)
    return (t32 * cos + rot * sin).astype(t.dtype)


def _causal_block_attention(q_f, k_f, v_f, cum_f, q_d, k_d, v_d, lam):
    B, Hf, S, Df = q_f.shape
    Hd, Dh = q_d.shape[1], q_d.shape[-1]
    Dv = v_d.shape[-1]
    nb = S // Q_BLOCK
    qf_blk = jnp.moveaxis(q_f.reshape(B, Hf, nb, Q_BLOCK, Df), 2, 0)
    cf_blk = jnp.moveaxis(cum_f.reshape(B, Hf, nb, Q_BLOCK), 2, 0)
    qd_blk = jnp.moveaxis(q_d.reshape(B, Hd, 2, nb, Q_BLOCK, Dh), 3, 0)
    k_pos = jnp.arange(S)
    scale_f = Df ** -0.5
    scale_d = Dh ** -0.5

    def one_block(args):
        qf, cf, qd, blk = args
        q_pos = blk * Q_BLOCK + jnp.arange(Q_BLOCK)
        causal = k_pos[None, :] <= q_pos[:, None]
        s_f = jnp.einsum('bhqd,bhkd->bhqk', qf, k_f).astype(jnp.float32) * scale_f
        s_f = s_f + cf[..., :, None] - cum_f[..., None, :]
        s_f = jnp.where(causal, s_f, -jnp.inf)
        p_f = jax.nn.softmax(s_f, axis=-1)
        o_f = jnp.einsum('bhqk,bhkd->bhqd', p_f.astype(v_f.dtype), v_f)
        s_d = jnp.einsum('bhcqd,bhckd->bhcqk', qd, k_d).astype(jnp.float32) * scale_d
        s_d = jnp.where(causal, s_d, -jnp.inf)
        p_d = jax.nn.softmax(s_d, axis=-1)
        a_d = p_d[:, :, 0] - lam * p_d[:, :, 1]
        o_d = jnp.einsum('bhqk,bhkd->bhqd', a_d.astype(v_d.dtype), v_d)
        return o_f, o_d

    o_f, o_d = lax.map(one_block, (qf_blk, cf_blk, qd_blk, jnp.arange(nb)))
    o_f = jnp.moveaxis(o_f, 0, 2).reshape(B, Hf, S, Df)
    o_d = jnp.moveaxis(o_d, 0, 2).reshape(B, Hd, S, Dv)
    return o_f, o_d


def _hybrid_mixer(h, w_in, forget_bias, lq1, lk1, lq2, lk2, subln, w_out, lambda_init):
    B, S, _ = h.shape
    proj = h @ w_in
    fq, fk, fv, ff, dq, dk, dv = jnp.split(proj, IN_OFFSETS, axis=-1)
    to_heads_f = lambda t: t.reshape(B, S, FOX_HEADS, FOX_HEAD_DIM).transpose(0, 2, 1, 3)
    q_f, k_f, v_f = to_heads_f(fq), to_heads_f(fk), to_heads_f(fv)
    log_f = jax.nn.log_sigmoid((ff + forget_bias).astype(jnp.float32))
    cum_f = jnp.cumsum(log_f.transpose(0, 2, 1), axis=-1)
    to_heads_d = lambda t: t.reshape(B, S, DIFF_HEADS, 2, DIFF_HEAD_DIM).transpose(0, 2, 3, 1, 4)
    cos, sin = _rope_tables(S, DIFF_HEAD_DIM)
    q_d = _apply_rope(to_heads_d(dq), cos, sin)
    k_d = _apply_rope(to_heads_d(dk), cos, sin)
    v_d = dv.reshape(B, S, DIFF_HEADS, DIFF_V_DIM).transpose(0, 2, 1, 3)
    lam = (jnp.exp(jnp.sum(lq1.astype(jnp.float32) * lk1.astype(jnp.float32)))
           - jnp.exp(jnp.sum(lq2.astype(jnp.float32) * lk2.astype(jnp.float32)))
           + lambda_init)
    o_f, o_d = _causal_block_attention(q_f, k_f, v_f, cum_f, q_d, k_d, v_d, lam)
    o_d = _rmsnorm(o_d, subln) * (1.0 - lambda_init)
    o_f = o_f.transpose(0, 2, 1, 3).reshape(B, S, FOX_WIDTH)
    o_d = o_d.transpose(0, 2, 1, 3).reshape(B, S, DIFF_WIDTH)
    return jnp.concatenate([o_f, o_d], axis=-1) @ w_out


def setup_inputs(seed: int = 0) -> dict:
    key = jax.random.key(seed)
    ks = jax.random.split(key, 20)
    nrm = lambda k, shape, fan_in: jax.random.normal(k, shape, jnp.float32) * fan_in ** -0.5
    gain = lambda k, shape: 1.0 + 0.02 * jax.random.normal(k, shape, jnp.float32)
    return {
        "x": jax.random.normal(ks[0], (BATCH, SEQ, D_MODEL), jnp.float32),
        "ffn1_norm": gain(ks[1], (DEPTH, D_MODEL)),
        "ffn1_w_gate_up": nrm(ks[2], (DEPTH, D_MODEL, 2 * D_FF), D_MODEL),
        "ffn1_w_down": nrm(ks[3], (DEPTH, D_FF, D_MODEL), D_FF),
        "mix_norm": gain(ks[4], (DEPTH, D_MODEL)),
        "w_in": nrm(ks[5], (DEPTH, D_MODEL, IN_COLS), D_MODEL),
        "forget_bias": jax.random.uniform(ks[6], (DEPTH, FOX_HEADS), jnp.float32, 1.0, 4.0),
        "lambda_q1": 0.1 * jax.random.normal(ks[7], (DEPTH, DIFF_HEAD_DIM), jnp.float32),
        "lambda_k1": 0.1 * jax.random.normal(ks[8], (DEPTH, DIFF_HEAD_DIM), jnp.float32),
        "lambda_q2": 0.1 * jax.random.normal(ks[9], (DEPTH, DIFF_HEAD_DIM), jnp.float32),
        "lambda_k2": 0.1 * jax.random.normal(ks[10], (DEPTH, DIFF_HEAD_DIM), jnp.float32),
        "diff_subln": gain(ks[11], (DEPTH, DIFF_V_DIM)),
        "w_out": nrm(ks[12], (DEPTH, MIX_WIDTH, D_MODEL), MIX_WIDTH),
        "ffn2_norm": gain(ks[13], (DEPTH, D_MODEL)),
        "ffn2_w_gate_up": nrm(ks[14], (DEPTH, D_MODEL, 2 * D_FF), D_MODEL),
        "ffn2_w_down": nrm(ks[15], (DEPTH, D_FF, D_MODEL), D_FF),
        "final_norm": gain(ks[16], (D_MODEL,)),
    }


def reference(x, ffn1_norm, ffn1_w_gate_up, ffn1_w_down, mix_norm, w_in, forget_bias,
              lambda_q1, lambda_k1, lambda_q2, lambda_k2, diff_subln, w_out,
              ffn2_norm, ffn2_w_gate_up, ffn2_w_down, final_norm):
    for layer in range(DEPTH):
        lambda_init = 0.8 - 0.6 * math.exp(-0.3 * layer)
        x = x + FFN_RES_WEIGHT * _swiglu(_rmsnorm(x, ffn1_norm[layer]),
                                         ffn1_w_gate_up[layer], ffn1_w_down[layer])
        x = x + _hybrid_mixer(_rmsnorm(x, mix_norm[layer]), w_in[layer], forget_bias[layer],
                              lambda_q1[layer], lambda_k1[layer], lambda_q2[layer],
                              lambda_k2[layer], diff_subln[layer], w_out[layer], lambda_init)
        x = x + FFN_RES_WEIGHT * _swiglu(_rmsnorm(x, ffn2_norm[layer]),
                                         ffn2_w_gate_up[layer], ffn2_w_down[layer])
    return _rmsnorm(x, final_norm)
```

```python
import functools
import math

import jax
import jax.numpy as jnp
from jax import lax
from jax.experimental import pallas as pl
from jax.experimental.pallas import tpu as pltpu

D_MODEL = 1024
DEPTH = 2
FOX_HEADS = 8
FOX_HEAD_DIM = 64
DIFF_HEADS = 4
DIFF_HEAD_DIM = 64
DIFF_V_DIM = 2 * DIFF_HEAD_DIM
FOX_WIDTH = FOX_HEADS * FOX_HEAD_DIM
DIFF_WIDTH = DIFF_HEADS * DIFF_V_DIM
D_FF = 2816
ROPE_THETA = 10000.0
NORM_EPS = 1e-5
FFN_RES_WEIGHT = 0.5

LOG2E = 1.4426950408889634
Q_SCALE = FOX_HEAD_DIM ** -0.5 * LOG2E
MASKED_LOGIT = -1e30

LANES = 128
BF16_SUBLANES = 16
VMEM_LIMIT_BYTES = 56 * 1024 * 1024

FFN_TOKENS = 512
FF_CHUNK = D_FF // 2
PROJ_TOKENS = 512
ATTN_BLOCK = 512

QK_WIDTH = LANES
FOX_V_ROWS = FOX_HEAD_DIM + BF16_SUBLANES
DIFF_V_ROWS = DIFF_V_DIM + BF16_SUBLANES
N_PROJ_GROUPS = 8
GROUP = 512

_f32 = jnp.float32
_bf16 = jnp.bfloat16


def _rmsnorm(x, gain):
    return x * lax.rsqrt(jnp.mean(x * x, axis=-1, keepdims=True) + NORM_EPS) * gain


def _mm(a, b):
    return jnp.dot(a, b, preferred_element_type=_f32)


def _split3(v):
    hi = v.astype(_bf16).astype(_f32)
    r = v - hi
    mid = r.astype(_bf16).astype(_f32)
    lo = (r - mid).astype(_bf16).astype(_f32)
    return hi, mid, lo


def _ffn_kernel(x_ref, gain_ref, wg_ref, wu_ref, wd_ref, fgain_ref, o_ref, *, final_norm):
    x = x_ref[...]
    h = _rmsnorm(x, gain_ref[...]).astype(_bf16)
    y = jnp.zeros_like(x)
    for c in range(D_FF // FF_CHUNK):
        cols = slice(c * FF_CHUNK, (c + 1) * FF_CHUNK)
        g = _mm(h, wg_ref[:, cols])
        u = _mm(h, wu_ref[:, cols])
        a = (g * jax.nn.sigmoid(g) * u).astype(_bf16)
        y = y + _mm(a, wd_ref[cols, :])
    out = x + FFN_RES_WEIGHT * y
    if final_norm:
        out = _rmsnorm(out, fgain_ref[...])
    o_ref[...] = out


def _ffn(x2d, gain, w_gate_up, w_down, final_gain, final_norm):
    tokens = x2d.shape[0]
    resident = dict(pipeline_mode=pl.Buffered(1))
    return pl.pallas_call(
        functools.partial(_ffn_kernel, final_norm=final_norm),
        grid=(tokens // FFN_TOKENS,),
        in_specs=[
            pl.BlockSpec((FFN_TOKENS, D_MODEL), lambda i: (i, 0)),
            pl.BlockSpec((1, D_MODEL), lambda i: (0, 0)),
            pl.BlockSpec((D_MODEL, D_FF), lambda i: (0, 0), **resident),
            pl.BlockSpec((D_MODEL, D_FF), lambda i: (0, 1), **resident),
            pl.BlockSpec((D_FF, D_MODEL), lambda i: (0, 0), **resident),
            pl.BlockSpec((1, D_MODEL), lambda i: (0, 0)),
        ],
        out_specs=pl.BlockSpec((FFN_TOKENS, D_MODEL), lambda i: (i, 0)),
        out_shape=jax.ShapeDtypeStruct((tokens, D_MODEL), _f32),
        compiler_params=pltpu.CompilerParams(
            dimension_semantics=("arbitrary",), vmem_limit_bytes=VMEM_LIMIT_BYTES),
        name="ffn",
    )(x2d, gain.reshape(1, D_MODEL), w_gate_up, w_gate_up, w_down,
      final_gain.reshape(1, D_MODEL))


def _inproj_kernel(x_ref, gain_ref, w_ref, wff_ref, fbias_ref, cos_ref, sin_ref,
                   p_ref, chi_ref, cmid_ref, clo_ref, carry_ref):
    tm = x_ref.shape[0]
    h = _rmsnorm(x_ref[...], gain_ref[...]).astype(_bf16)
    proj = _mm(h, w_ref[...])
    fq, fk, fv, dq, dk, dv, dq_rot, dk_rot = [
        proj[:, n * GROUP:(n + 1) * GROUP] for n in range(N_PROJ_GROUPS)]
    cos = jnp.concatenate([cos_ref[...]] * (GROUP // LANES), axis=1)
    sin = jnp.concatenate([sin_ref[...]] * (GROUP // LANES), axis=1)
    dq = (dq * cos + dq_rot * sin) * Q_SCALE
    dk = dk * cos + dk_rot * sin
    fq = fq * Q_SCALE
    for n, t in enumerate((fq, fk, fv, dq, dk, dv)):
        p_ref[:, n * GROUP:(n + 1) * GROUP] = t.astype(_bf16)

    gate = _mm(h, wff_ref[...]) + fbias_ref[...]
    log_f = jnp.minimum(gate, 0.0) - jnp.log1p(jnp.exp(-jnp.abs(gate)))
    row = lax.broadcasted_iota(jnp.int32, (tm, tm), 0)
    col = lax.broadcasted_iota(jnp.int32, (tm, tm), 1)
    tri = jnp.where(col <= row, 1.0, 0.0).astype(_bf16)
    hi, mid, lo = _split3(log_f)
    cum = (_mm(tri, hi.astype(_bf16)) + _mm(tri, mid.astype(_bf16))
           + _mm(tri, lo.astype(_bf16)))

    @pl.when(pl.program_id(1) == 0)
    def _():
        carry_ref[...] = jnp.zeros_like(carry_ref)

    c = cum + carry_ref[0:1, :]
    carry_ref[0:1, :] = c[tm - 1:tm, :]
    c_hi, c_mid, c_lo = _split3(c * LOG2E)
    chi_ref[...] = c_hi[:, :FOX_HEADS]
    cmid_ref[...] = c_mid[:, :FOX_HEADS]
    clo_ref[...] = c_lo[:, :FOX_HEADS]


def _inproj(x3d, gain, w_all, w_ff, fbias, cos2, sin2):
    batch, seq, _ = x3d.shape
    tm = min(PROJ_TOKENS, seq)
    resident = dict(pipeline_mode=pl.Buffered(1))
    c_shape = jax.ShapeDtypeStruct((batch, seq, FOX_HEADS), _f32)
    c_spec = pl.BlockSpec((None, tm, FOX_HEADS), lambda b, i: (b, i, 0))
    return pl.pallas_call(
        _inproj_kernel,
        grid=(batch, seq // tm),
        in_specs=[
            pl.BlockSpec((None, tm, D_MODEL), lambda b, i: (b, i, 0)),
            pl.BlockSpec((1, D_MODEL), lambda b, i: (0, 0)),
            pl.BlockSpec((D_MODEL, N_PROJ_GROUPS * GROUP), lambda b, i: (0, 0), **resident),
            pl.BlockSpec((D_MODEL, LANES), lambda b, i: (0, 0), **resident),
            pl.BlockSpec((1, LANES), lambda b, i: (0, 0)),
            pl.BlockSpec((tm, LANES), lambda b, i: (i, 0)),
            pl.BlockSpec((tm, LANES), lambda b, i: (i, 0)),
        ],
        out_specs=[
            pl.BlockSpec((None, tm, 6 * GROUP), lambda b, i: (b, i, 0)),
            c_spec, c_spec, c_spec,
        ],
        out_shape=[
            jax.ShapeDtypeStruct((batch, seq, 6 * GROUP), _bf16),
            c_shape, c_shape, c_shape,
        ],
        scratch_shapes=[pltpu.VMEM((8, LANES), _f32)],
        compiler_params=pltpu.CompilerParams(
            dimension_semantics=("arbitrary", "arbitrary"),
            vmem_limit_bytes=VMEM_LIMIT_BYTES),
        name="inproj",
    )(x3d, gain.reshape(1, D_MODEL), w_all, w_ff, fbias, cos2, sin2)


def _attn_kernel(*refs, n_comp, dv, lambda_init):
    if n_comp == 1:
        qT_ref, k_ref, vT_ref, o_ref, m_ref, acc_ref = refs
    else:
        (qT_ref, k_ref, vT_ref, lq1_ref, lk1_ref, lq2_ref, lk2_ref, subln_ref,
         o_ref, m_ref, acc_ref) = refs
    tk = k_ref.shape[1]
    tq = qT_ref.shape[2]
    qi = pl.program_id(1)
    m_ref[...] = jnp.full(m_ref.shape, MASKED_LOGIT, _f32)
    acc_ref[...] = jnp.zeros(acc_ref.shape, _f32)

    def block(j, masked):
        k_blk = k_ref[j]
        v_blk = vT_ref[j]
        for c in range(n_comp):
            s = _mm(k_blk, qT_ref[c])
            if masked:
                key = lax.broadcasted_iota(jnp.int32, (tk, tq), 0)
                qry = lax.broadcasted_iota(jnp.int32, (tk, tq), 1)
                s = jnp.where(key <= qry, s, MASKED_LOGIT)
            m_old = m_ref[c]
            m_new = jnp.maximum(m_old, jnp.max(s, axis=0, keepdims=True))
            alpha = jnp.exp2(m_old - m_new)
            p = jnp.exp2(s - m_new).astype(_bf16)
            acc_ref[c] = alpha * acc_ref[c] + _mm(v_blk, p)
            m_ref[c] = m_new

    def body(j, carry):
        block(j, False)
        return carry

    lax.fori_loop(0, qi, body, 0)
    block(qi, True)

    if n_comp == 1:
        acc = acc_ref[0]
        o_ref[...] = (acc[:dv, :] / acc[dv:dv + 1, :]).astype(o_ref.dtype)
    else:
        a1 = acc_ref[0]
        a2 = acc_ref[1]
        lam = (jnp.exp(jnp.sum(lq1_ref[...] * lk1_ref[...], axis=1, keepdims=True))
               - jnp.exp(jnp.sum(lq2_ref[...] * lk2_ref[...], axis=1, keepdims=True))
               + lambda_init)
        o = a1[:dv, :] / a1[dv:dv + 1, :] - lam * (a2[:dv, :] / a2[dv:dv + 1, :])
        o = o * lax.rsqrt(jnp.mean(o * o, axis=0, keepdims=True) + NORM_EPS)
        o_ref[...] = (o * subln_ref[...] * (1.0 - lambda_init)).astype(o_ref.dtype)


def _attention(qT, k, vT, extras, *, n_comp, dv, lambda_init):
    units, _, _, seq = qT.shape
    n_blocks, tk = k.shape[1], k.shape[2]
    v_rows = vT.shape[2]
    tq = tk
    extra_specs = [pl.BlockSpec(e.shape, lambda u, i: (0, 0)) for e in extras]
    return pl.pallas_call(
        functools.partial(_attn_kernel, n_comp=n_comp, dv=dv, lambda_init=lambda_init),
        grid=(units, seq // tq),
        in_specs=[
            pl.BlockSpec((None, n_comp, QK_WIDTH, tq), lambda u, i: (u, 0, 0, i)),
            pl.BlockSpec((None, n_blocks, tk, QK_WIDTH), lambda u, i: (u, 0, 0, 0)),
            pl.BlockSpec((None, n_blocks, v_rows, tk), lambda u, i: (u, 0, 0, 0)),
        ] + extra_specs,
        out_specs=pl.BlockSpec((None, dv, tq), lambda u, i: (u, 0, i)),
        out_shape=jax.ShapeDtypeStruct((units, dv, seq), _bf16),
        scratch_shapes=[
            pltpu.VMEM((n_comp, 1, tq), _f32),
            pltpu.VMEM((n_comp, v_rows, tq), _f32),
        ],
        compiler_params=pltpu.CompilerParams(
            dimension_semantics=("arbitrary", "arbitrary"),
            vmem_limit_bytes=VMEM_LIMIT_BYTES),
        name="fox_attn" if n_comp == 1 else "diff_attn",
    )(qT, k, vT, *extras)


def _outproj_kernel(x_ref, of_ref, od_ref, wf_ref, wd_ref, o_ref):
    contract_rows = (((0,), (0,)), ((), ()))
    y = lax.dot_general(of_ref[...], wf_ref[...], contract_rows, preferred_element_type=_f32)
    y = y + lax.dot_general(od_ref[...], wd_ref[...], contract_rows, preferred_element_type=_f32)
    o_ref[...] = x_ref[...] + y


def _outproj(x3d, oT_fox, oT_diff, w_out):
    batch, seq, _ = x3d.shape
    tm = min(PROJ_TOKENS, seq)
    resident = dict(pipeline_mode=pl.Buffered(1))
    return pl.pallas_call(
        _outproj_kernel,
        grid=(batch, seq // tm),
        in_specs=[
            pl.BlockSpec((None, tm, D_MODEL), lambda b, i: (b, i, 0)),
            pl.BlockSpec((None, FOX_WIDTH, tm), lambda b, i: (b, 0, i)),
            pl.BlockSpec((None, DIFF_WIDTH, tm), lambda b, i: (b, 0, i)),
            pl.BlockSpec((FOX_WIDTH, D_MODEL), lambda b, i: (0, 0), **resident),
            pl.BlockSpec((DIFF_WIDTH, D_MODEL), lambda b, i: (1, 0), **resident),
        ],
        out_specs=pl.BlockSpec((None, tm, D_MODEL), lambda b, i: (b, i, 0)),
        out_shape=jax.ShapeDtypeStruct((batch, seq, D_MODEL), _f32),
        compiler_params=pltpu.CompilerParams(
            dimension_semantics=("arbitrary", "arbitrary"),
            vmem_limit_bytes=VMEM_LIMIT_BYTES),
        name="outproj",
    )(x3d, oT_fox, oT_diff, w_out, w_out)


def _rotate_half_columns(w):
    g = w.reshape(w.shape[0], -1, DIFF_HEAD_DIM)
    half = DIFF_HEAD_DIM // 2
    return jnp.concatenate([-g[..., half:], g[..., :half]], axis=-1).reshape(w.shape)


def _prep_w_in(w_in):
    o = 0
    parts = {}
    for name, width in (("fq", FOX_WIDTH), ("fk", FOX_WIDTH), ("fv", FOX_WIDTH),
                        ("ff", FOX_HEADS), ("dq", GROUP), ("dk", GROUP), ("dv", DIFF_WIDTH)):
        parts[name] = w_in[:, o:o + width]
        o += width
    w_all = jnp.concatenate(
        [parts["fq"], parts["fk"], parts["fv"], parts["dq"], parts["dk"], parts["dv"],
         _rotate_half_columns(parts["dq"]), _rotate_half_columns(parts["dk"])], axis=1)
    w_ff = jnp.pad(parts["ff"], ((0, 0), (0, LANES - FOX_HEADS)))
    return w_all.astype(_bf16), w_ff.astype(_bf16)


def _rope_tables(seq):
    inv_freq = 1.0 / (ROPE_THETA ** (jnp.arange(0, DIFF_HEAD_DIM, 2, dtype=_f32) / DIFF_HEAD_DIM))
    freqs = jnp.arange(seq, dtype=_f32)[:, None] * inv_freq[None, :]
    emb = jnp.concatenate([freqs, freqs, freqs, freqs], axis=-1)
    return jnp.cos(emb), jnp.sin(emb)


def _fox_layouts(p, c_parts, tk):
    batch, seq, _ = p.shape
    nb = seq // tk
    units = batch * FOX_HEADS
    fq, fk, fv = (p[..., n * GROUP:(n + 1) * GROUP] for n in range(3))
    c_rows = [c.astype(_bf16).transpose(0, 2, 1)[:, :, None, :] for c in c_parts]
    c_cols = [c.astype(_bf16)[..., None] for c in c_parts]
    n_aux = 2 * len(c_parts)
    qT = jnp.concatenate(
        [fq.reshape(batch, seq, FOX_HEADS, FOX_HEAD_DIM).transpose(0, 2, 3, 1)]
        + c_rows
        + [jnp.full((batch, FOX_HEADS, len(c_parts), seq), -1.0, _bf16),
           jnp.zeros((batch, FOX_HEADS, QK_WIDTH - FOX_HEAD_DIM - n_aux, seq), _bf16)], axis=2)
    k = jnp.concatenate(
        [fk.reshape(batch, seq, FOX_HEADS, FOX_HEAD_DIM),
         jnp.ones((batch, seq, FOX_HEADS, len(c_parts)), _bf16)]
        + c_cols
        + [jnp.zeros((batch, seq, FOX_HEADS, QK_WIDTH - FOX_HEAD_DIM - n_aux), _bf16)], axis=3)
    k = k.transpose(0, 2, 1, 3)
    vT = fv.reshape(batch, nb, tk, FOX_HEADS, FOX_HEAD_DIM).transpose(0, 3, 1, 4, 2)
    vT = jnp.concatenate(
        [vT, jnp.ones((batch, FOX_HEADS, nb, 1, tk), _bf16),
         jnp.zeros((batch, FOX_HEADS, nb, BF16_SUBLANES - 1, tk), _bf16)], axis=3)
    return (qT.reshape(units, 1, QK_WIDTH, seq), k.reshape(units, nb, tk, QK_WIDTH),
            vT.reshape(units, nb, FOX_V_ROWS, tk))


def _diff_layouts(p, tk):
    batch, seq, _ = p.shape
    nb = seq // tk
    units = batch * DIFF_HEADS
    dq, dk, dv = (p[..., n * GROUP:(n + 1) * GROUP] for n in range(3, 6))
    dqT = dq.reshape(batch, seq, DIFF_HEADS, 2, DIFF_HEAD_DIM).transpose(0, 2, 3, 4, 1)
    z = jnp.zeros((batch, DIFF_HEADS, DIFF_HEAD_DIM, seq), _bf16)
    qT = jnp.stack([jnp.concatenate([dqT[:, :, 0], z], axis=2),
                    jnp.concatenate([z, dqT[:, :, 1]], axis=2)], axis=2)
    k = dk.reshape(batch, seq, DIFF_HEADS, 2 * DIFF_HEAD_DIM).transpose(0, 2, 1, 3)
    vT = dv.reshape(batch, nb, tk, DIFF_HEADS, DIFF_V_DIM).transpose(0, 3, 1, 4, 2)
    vT = jnp.concatenate(
        [vT, jnp.ones((batch, DIFF_HEADS, nb, 1, tk), _bf16),
         jnp.zeros((batch, DIFF_HEADS, nb, BF16_SUBLANES - 1, tk), _bf16)], axis=3)
    return (qT.reshape(units, 2, QK_WIDTH, seq), k.reshape(units, nb, tk, QK_WIDTH),
            vT.reshape(units, nb, DIFF_V_ROWS, tk))


def kernel(x, ffn1_norm, ffn1_w_gate_up, ffn1_w_down, mix_norm, w_in, forget_bias,
           lambda_q1, lambda_k1, lambda_q2, lambda_k2, diff_subln, w_out,
           ffn2_norm, ffn2_w_gate_up, ffn2_w_down, final_norm):
    batch, seq, _ = x.shape
    tk = min(ATTN_BLOCK, seq)
    cos2, sin2 = _rope_tables(seq)
    for layer in range(DEPTH):
        lambda_init = 0.8 - 0.6 * math.exp(-0.3 * layer)
        x2d = _ffn(x.reshape(batch * seq, D_MODEL), ffn1_norm[layer],
                   ffn1_w_gate_up[layer].astype(_bf16), ffn1_w_down[layer].astype(_bf16),
                   final_norm, False)
        x = x2d.reshape(batch, seq, D_MODEL)

        w_all, w_ff = _prep_w_in(w_in[layer])
        fbias = jnp.pad(forget_bias[layer], (0, LANES - FOX_HEADS)).reshape(1, LANES)
        p, c_hi, c_mid, c_lo = _inproj(x, mix_norm[layer], w_all, w_ff, fbias, cos2, sin2)

        oT_fox = _attention(*_fox_layouts(p, (c_hi, c_mid, c_lo), tk), (),
                            n_comp=1, dv=FOX_HEAD_DIM, lambda_init=lambda_init)
        lam_vecs = tuple(v[layer].reshape(1, DIFF_HEAD_DIM)
                         for v in (lambda_q1, lambda_k1, lambda_q2, lambda_k2))
        oT_diff = _attention(*_diff_layouts(p, tk),
                             lam_vecs + (diff_subln[layer].reshape(DIFF_V_DIM, 1),),
                             n_comp=2, dv=DIFF_V_DIM, lambda_init=lambda_init)
        x = _outproj(x, oT_fox.reshape(batch, FOX_WIDTH, seq),
                     oT_diff.reshape(batch, DIFF_WIDTH, seq), w_out[layer].astype(_bf16))

        x2d = _ffn(x.reshape(batch * seq, D_MODEL), ffn2_norm[layer],
                   ffn2_w_gate_up[layer].astype(_bf16), ffn2_w_down[layer].astype(_bf16),
                   final_norm, layer == DEPTH - 1)
        x = x2d.reshape(batch, seq, D_MODEL)
    return x
```

```python
import functools
import math

import jax
import jax.numpy as jnp
from jax import lax
from jax.experimental import pallas as pl
from jax.experimental.pallas import tpu as pltpu

D_MODEL = 1024
DEPTH = 2
FOX_HEADS = 8
FOX_HEAD_DIM = 64
DIFF_HEADS = 4
DIFF_HEAD_DIM = 64
DIFF_V_DIM = 2 * DIFF_HEAD_DIM
FOX_WIDTH = FOX_HEADS * FOX_HEAD_DIM
DIFF_WIDTH = DIFF_HEADS * DIFF_V_DIM
D_FF = 2816
ROPE_THETA = 10000.0
NORM_EPS = 1e-5
FFN_RES_WEIGHT = 0.5

LOG2E = 1.4426950408889634
Q_SCALE = FOX_HEAD_DIM ** -0.5 * LOG2E
MASKED_LOGIT = -1e30

LANES = 128
BF16_SUBLANES = 16
VMEM_LIMIT_BYTES = 56 * 1024 * 1024

FFN_TOKENS = 512
FF_CHUNK = D_FF // 2
PROJ_TOKENS = 512
ATTN_BLOCK = 1024

QK_WIDTH = LANES
FOX_V_ROWS = FOX_HEAD_DIM + BF16_SUBLANES
DIFF_V_ROWS = DIFF_V_DIM + BF16_SUBLANES
N_PROJ_GROUPS = 8
GROUP = 512

_f32 = jnp.float32
_bf16 = jnp.bfloat16


def _rmsnorm(x, gain):
    return x * lax.rsqrt(jnp.mean(x * x, axis=-1, keepdims=True) + NORM_EPS) * gain


def _mm(a, b):
    return jnp.dot(a, b, preferred_element_type=_f32)


def _split3(v):
    hi = v.astype(_bf16).astype(_f32)
    r = v - hi
    mid = r.astype(_bf16).astype(_f32)
    lo = (r - mid).astype(_bf16).astype(_f32)
    return hi, mid, lo


def _ffn_kernel(x_ref, gain_ref, wg_ref, wu_ref, wd_ref, fgain_ref, o_ref, *, final_norm):
    x = x_ref[...]
    h = _rmsnorm(x, gain_ref[...]).astype(_bf16)
    y = jnp.zeros_like(x)
    for c in range(D_FF // FF_CHUNK):
        cols = slice(c * FF_CHUNK, (c + 1) * FF_CHUNK)
        g = _mm(h, wg_ref[:, cols])
        u = _mm(h, wu_ref[:, cols])
        a = (g * jax.nn.sigmoid(g) * u).astype(_bf16)
        y = y + _mm(a, wd_ref[cols, :])
    out = x + FFN_RES_WEIGHT * y
    if final_norm:
        out = _rmsnorm(out, fgain_ref[...])
    o_ref[...] = out


def _ffn(x2d, gain, w_gate_up, w_down, final_gain, final_norm):
    tokens = x2d.shape[0]
    resident = dict(pipeline_mode=pl.Buffered(1))
    return pl.pallas_call(
        functools.partial(_ffn_kernel, final_norm=final_norm),
        grid=(tokens // FFN_TOKENS,),
        in_specs=[
            pl.BlockSpec((FFN_TOKENS, D_MODEL), lambda i: (i, 0)),
            pl.BlockSpec((1, D_MODEL), lambda i: (0, 0)),
            pl.BlockSpec((D_MODEL, D_FF), lambda i: (0, 0), **resident),
            pl.BlockSpec((D_MODEL, D_FF), lambda i: (0, 1), **resident),
            pl.BlockSpec((D_FF, D_MODEL), lambda i: (0, 0), **resident),
            pl.BlockSpec((1, D_MODEL), lambda i: (0, 0)),
        ],
        out_specs=pl.BlockSpec((FFN_TOKENS, D_MODEL), lambda i: (i, 0)),
        out_shape=jax.ShapeDtypeStruct((tokens, D_MODEL), _f32),
        compiler_params=pltpu.CompilerParams(
            dimension_semantics=("arbitrary",), vmem_limit_bytes=VMEM_LIMIT_BYTES),
        name="ffn",
    )(x2d, gain.reshape(1, D_MODEL), w_gate_up, w_gate_up, w_down,
      final_gain.reshape(1, D_MODEL))


def _inproj_kernel(x_ref, gain_ref, w_ref, wff_ref, fbias_ref, cos_ref, sin_ref,
                   p_ref, chi_ref, cmid_ref, clo_ref, carry_ref):
    tm = x_ref.shape[0]
    h = _rmsnorm(x_ref[...], gain_ref[...]).astype(_bf16)
    proj = _mm(h, w_ref[...])
    fq, fk, fv, dq, dk, dv, dq_rot, dk_rot = [
        proj[:, n * GROUP:(n + 1) * GROUP] for n in range(N_PROJ_GROUPS)]
    cos = jnp.concatenate([cos_ref[...]] * (GROUP // LANES), axis=1)
    sin = jnp.concatenate([sin_ref[...]] * (GROUP // LANES), axis=1)
    dq = (dq * cos + dq_rot * sin) * Q_SCALE
    dk = dk * cos + dk_rot * sin
    fq = fq * Q_SCALE
    for n, t in enumerate((fq, fk, fv, dq, dk, dv)):
        p_ref[:, n * GROUP:(n + 1) * GROUP] = t.astype(_bf16)

    gate = _mm(h, wff_ref[...]) + fbias_ref[...]
    log_f = jnp.minimum(gate, 0.0) - jnp.log1p(jnp.exp(-jnp.abs(gate)))
    row = lax.broadcasted_iota(jnp.int32, (tm, tm), 0)
    col = lax.broadcasted_iota(jnp.int32, (tm, tm), 1)
    tri = jnp.where(col <= row, 1.0, 0.0).astype(_bf16)
    hi, mid, lo = _split3(log_f)
    cum = (_mm(tri, hi.astype(_bf16)) + _mm(tri, mid.astype(_bf16))
           + _mm(tri, lo.astype(_bf16)))

    @pl.when(pl.program_id(1) == 0)
    def _():
        carry_ref[...] = jnp.zeros_like(carry_ref)

    c = cum + carry_ref[0:1, :]
    carry_ref[0:1, :] = c[tm - 1:tm, :]
    c_hi, c_mid, c_lo = _split3(c * LOG2E)
    chi_ref[...] = c_hi[:, :FOX_HEADS]
    cmid_ref[...] = c_mid[:, :FOX_HEADS]
    clo_ref[...] = c_lo[:, :FOX_HEADS]


def _inproj(x3d, gain, w_all, w_ff, fbias, cos2, sin2):
    batch, seq, _ = x3d.shape
    tm = min(PROJ_TOKENS, seq)
    resident = dict(pipeline_mode=pl.Buffered(1))
    c_shape = jax.ShapeDtypeStruct((batch, seq, FOX_HEADS), _f32)
    c_spec = pl.BlockSpec((None, tm, FOX_HEADS), lambda b, i: (b, i, 0))
    return pl.pallas_call(
        _inproj_kernel,
        grid=(batch, seq // tm),
        in_specs=[
            pl.BlockSpec((None, tm, D_MODEL), lambda b, i: (b, i, 0)),
            pl.BlockSpec((1, D_MODEL), lambda b, i: (0, 0)),
            pl.BlockSpec((D_MODEL, N_PROJ_GROUPS * GROUP), lambda b, i: (0, 0), **resident),
            pl.BlockSpec((D_MODEL, LANES), lambda b, i: (0, 0), **resident),
            pl.BlockSpec((1, LANES), lambda b, i: (0, 0)),
            pl.BlockSpec((tm, LANES), lambda b, i: (i, 0)),
            pl.BlockSpec((tm, LANES), lambda b, i: (i, 0)),
        ],
        out_specs=[
            pl.BlockSpec((None, tm, 6 * GROUP), lambda b, i: (b, i, 0)),
            c_spec, c_spec, c_spec,
        ],
        out_shape=[
            jax.ShapeDtypeStruct((batch, seq, 6 * GROUP), _bf16),
            c_shape, c_shape, c_shape,
        ],
        scratch_shapes=[pltpu.VMEM((8, LANES), _f32)],
        compiler_params=pltpu.CompilerParams(
            dimension_semantics=("arbitrary", "arbitrary"),
            vmem_limit_bytes=VMEM_LIMIT_BYTES),
        name="inproj",
    )(x3d, gain.reshape(1, D_MODEL), w_all, w_ff, fbias, cos2, sin2)


def _attn_kernel(*refs, n_comp, dv, lambda_init):
    if n_comp == 1:
        qT_ref, k_ref, vT_ref, o_ref = refs[:4]
    else:
        (qT_ref, k_ref, vT_ref, lq1_ref, lk1_ref, lq2_ref, lk2_ref, subln_ref,
         o_ref) = refs[:9]
    m_ref, acc_ref, s_even, s_odd, mb_even, mb_odd = refs[-6:]
    tk = k_ref.shape[1]
    tq = qT_ref.shape[2]
    qi = pl.program_id(1)
    m_ref[...] = jnp.full(m_ref.shape, MASKED_LOGIT, _f32)
    acc_ref[...] = jnp.zeros(acc_ref.shape, _f32)

    def produce(j, s_ref, mb_ref):
        k_blk = k_ref[j]
        for c in range(n_comp):
            s = _mm(k_blk, qT_ref[c])
            s_ref[c] = s
            mb_ref[c] = jnp.max(s, axis=0, keepdims=True)

    def consume(j, s_ref, mb_ref, diagonal):
        v_blk = vT_ref[j]
        for c in range(n_comp):
            s = s_ref[c]
            if diagonal:
                key = lax.broadcasted_iota(jnp.int32, (tk, tq), 0)
                qry = lax.broadcasted_iota(jnp.int32, (tk, tq), 1)
                s = jnp.where(key <= qry, s, MASKED_LOGIT)
                m_blk = jnp.max(s, axis=0, keepdims=True)
            else:
                m_blk = mb_ref[c]
            m_old = m_ref[c]
            m_new = jnp.maximum(m_old, m_blk)
            alpha = jnp.exp2(m_old - m_new)
            p = jnp.exp2(s - m_new).astype(_bf16)
            acc_ref[c] = alpha * acc_ref[c] + _mm(v_blk, p)
            m_ref[c] = m_new

    produce(0, s_even, mb_even)

    def pair(i, carry):
        j = 2 * i
        produce(j + 1, s_odd, mb_odd)
        consume(j, s_even, mb_even, False)
        produce(j + 2, s_even, mb_even)
        consume(j + 1, s_odd, mb_odd, False)
        return carry

    lax.fori_loop(0, qi // 2, pair, 0)

    @pl.when(qi % 2 == 1)
    def _():
        produce(qi, s_odd, mb_odd)
        consume(qi - 1, s_even, mb_even, False)
        consume(qi, s_odd, mb_odd, True)

    @pl.when(qi % 2 == 0)
    def _():
        consume(qi, s_even, mb_even, True)

    if n_comp == 1:
        acc = acc_ref[0]
        o_ref[...] = (acc[:dv, :] / acc[dv:dv + 1, :]).astype(o_ref.dtype)
    else:
        a1 = acc_ref[0]
        a2 = acc_ref[1]
        lam = (jnp.exp(jnp.sum(lq1_ref[...] * lk1_ref[...], axis=1, keepdims=True))
               - jnp.exp(jnp.sum(lq2_ref[...] * lk2_ref[...], axis=1, keepdims=True))
               + lambda_init)
        o = a1[:dv, :] / a1[dv:dv + 1, :] - lam * (a2[:dv, :] / a2[dv:dv + 1, :])
        o = o * lax.rsqrt(jnp.mean(o * o, axis=0, keepdims=True) + NORM_EPS)
        o_ref[...] = (o * subln_ref[...] * (1.0 - lambda_init)).astype(o_ref.dtype)


def _attention(qT, k, vT, extras, *, n_comp, dv, lambda_init):
    units, _, _, seq = qT.shape
    n_blocks, tk = k.shape[1], k.shape[2]
    v_rows = vT.shape[2]
    tq = tk
    extra_specs = [pl.BlockSpec(e.shape, lambda u, i: (0, 0)) for e in extras]
    return pl.pallas_call(
        functools.partial(_attn_kernel, n_comp=n_comp, dv=dv, lambda_init=lambda_init),
        grid=(units, seq // tq),
        in_specs=[
            pl.BlockSpec((None, n_comp, QK_WIDTH, tq), lambda u, i: (u, 0, 0, i)),
            pl.BlockSpec((None, n_blocks, tk, QK_WIDTH), lambda u, i: (u, 0, 0, 0)),
            pl.BlockSpec((None, n_blocks, v_rows, tk), lambda u, i: (u, 0, 0, 0)),
        ] + extra_specs,
        out_specs=pl.BlockSpec((None, dv, tq), lambda u, i: (u, 0, i)),
        out_shape=jax.ShapeDtypeStruct((units, dv, seq), _bf16),
        scratch_shapes=[
            pltpu.VMEM((n_comp, 1, tq), _f32),
            pltpu.VMEM((n_comp, v_rows, tq), _f32),
            pltpu.VMEM((n_comp, tk, tq), _f32),
            pltpu.VMEM((n_comp, tk, tq), _f32),
            pltpu.VMEM((n_comp, 1, tq), _f32),
            pltpu.VMEM((n_comp, 1, tq), _f32),
        ],
        compiler_params=pltpu.CompilerParams(
            dimension_semantics=("arbitrary", "arbitrary"),
            vmem_limit_bytes=VMEM_LIMIT_BYTES),
        name="fox_attn" if n_comp == 1 else "diff_attn",
    )(qT, k, vT, *extras)


def _outproj_kernel(x_ref, of_ref, od_ref, wf_ref, wd_ref, o_ref):
    contract_rows = (((0,), (0,)), ((), ()))
    y = lax.dot_general(of_ref[...], wf_ref[...], contract_rows, preferred_element_type=_f32)
    y = y + lax.dot_general(od_ref[...], wd_ref[...], contract_rows, preferred_element_type=_f32)
    o_ref[...] = x_ref[...] + y


def _outproj(x3d, oT_fox, oT_diff, w_out):
    batch, seq, _ = x3d.shape
    tm = min(PROJ_TOKENS, seq)
    resident = dict(pipeline_mode=pl.Buffered(1))
    return pl.pallas_call(
        _outproj_kernel,
        grid=(batch, seq // tm),
        in_specs=[
            pl.BlockSpec((None, tm, D_MODEL), lambda b, i: (b, i, 0)),
            pl.BlockSpec((None, FOX_WIDTH, tm), lambda b, i: (b, 0, i)),
            pl.BlockSpec((None, DIFF_WIDTH, tm), lambda b, i: (b, 0, i)),
            pl.BlockSpec((FOX_WIDTH, D_MODEL), lambda b, i: (0, 0), **resident),
            pl.BlockSpec((DIFF_WIDTH, D_MODEL), lambda b, i: (1, 0), **resident),
        ],
        out_specs=pl.BlockSpec((None, tm, D_MODEL), lambda b, i: (b, i, 0)),
        out_shape=jax.ShapeDtypeStruct((batch, seq, D_MODEL), _f32),
        compiler_params=pltpu.CompilerParams(
            dimension_semantics=("arbitrary", "arbitrary"),
            vmem_limit_bytes=VMEM_LIMIT_BYTES),
        name="outproj",
    )(x3d, oT_fox, oT_diff, w_out, w_out)


def _rotate_half_columns(w):
    g = w.reshape(w.shape[0], -1, DIFF_HEAD_DIM)
    half = DIFF_HEAD_DIM // 2
    return jnp.concatenate([-g[..., half:], g[..., :half]], axis=-1).reshape(w.shape)


def _prep_w_in(w_in):
    o = 0
    parts = {}
    for name, width in (("fq", FOX_WIDTH), ("fk", FOX_WIDTH), ("fv", FOX_WIDTH),
                        ("ff", FOX_HEADS), ("dq", GROUP), ("dk", GROUP), ("dv", DIFF_WIDTH)):
        parts[name] = w_in[:, o:o + width]
        o += width
    w_all = jnp.concatenate(
        [parts["fq"], parts["fk"], parts["fv"], parts["dq"], parts["dk"], parts["dv"],
         _rotate_half_columns(parts["dq"]), _rotate_half_columns(parts["dk"])], axis=1)
    w_ff = jnp.pad(parts["ff"], ((0, 0), (0, LANES - FOX_HEADS)))
    return w_all.astype(_bf16), w_ff.astype(_bf16)


def _rope_tables(seq):
    inv_freq = 1.0 / (ROPE_THETA ** (jnp.arange(0, DIFF_HEAD_DIM, 2, dtype=_f32) / DIFF_HEAD_DIM))
    freqs = jnp.arange(seq, dtype=_f32)[:, None] * inv_freq[None, :]
    emb = jnp.concatenate([freqs, freqs, freqs, freqs], axis=-1)
    return jnp.cos(emb), jnp.sin(emb)


def _fox_layouts(p, c_parts, tk):
    batch, seq, _ = p.shape
    nb = seq // tk
    units = batch * FOX_HEADS
    fq, fk, fv = (p[..., n * GROUP:(n + 1) * GROUP] for n in range(3))
    c_rows = [c.astype(_bf16).transpose(0, 2, 1)[:, :, None, :] for c in c_parts]
    c_cols = [c.astype(_bf16)[..., None] for c in c_parts]
    n_aux = 2 * len(c_parts)
    qT = jnp.concatenate(
        [fq.reshape(batch, seq, FOX_HEADS, FOX_HEAD_DIM).transpose(0, 2, 3, 1)]
        + c_rows
        + [jnp.full((batch, FOX_HEADS, len(c_parts), seq), -1.0, _bf16),
           jnp.zeros((batch, FOX_HEADS, QK_WIDTH - FOX_HEAD_DIM - n_aux, seq), _bf16)], axis=2)
    k = jnp.concatenate(
        [fk.reshape(batch, seq, FOX_HEADS, FOX_HEAD_DIM),
         jnp.ones((batch, seq, FOX_HEADS, len(c_parts)), _bf16)]
        + c_cols
        + [jnp.zeros((batch, seq, FOX_HEADS, QK_WIDTH - FOX_HEAD_DIM - n_aux), _bf16)], axis=3)
    k = k.transpose(0, 2, 1, 3)
    vT = fv.reshape(batch, nb, tk, FOX_HEADS, FOX_HEAD_DIM).transpose(0, 3, 1, 4, 2)
    vT = jnp.concatenate(
        [vT, jnp.ones((batch, FOX_HEADS, nb, 1, tk), _bf16),
         jnp.zeros((batch, FOX_HEADS, nb, BF16_SUBLANES - 1, tk), _bf16)], axis=3)
    return (qT.reshape(units, 1, QK_WIDTH, seq), k.reshape(units, nb, tk, QK_WIDTH),
            vT.reshape(units, nb, FOX_V_ROWS, tk))


def _diff_layouts(p, tk):
    batch, seq, _ = p.shape
    nb = seq // tk
    units = batch * DIFF_HEADS
    dq, dk, dv = (p[..., n * GROUP:(n + 1) * GROUP] for n in range(3, 6))
    dqT = dq.reshape(batch, seq, DIFF_HEADS, 2, DIFF_HEAD_DIM).transpose(0, 2, 3, 4, 1)
    z = jnp.zeros((batch, DIFF_HEADS, DIFF_HEAD_DIM, seq), _bf16)
    qT = jnp.stack([jnp.concatenate([dqT[:, :, 0], z], axis=2),
                    jnp.concatenate([z, dqT[:, :, 1]], axis=2)], axis=2)
    k = dk.reshape(batch, seq, DIFF_HEADS, 2 * DIFF_HEAD_DIM).transpose(0, 2, 1, 3)
    vT = dv.reshape(batch, nb, tk, DIFF_HEADS, DIFF_V_DIM).transpose(0, 3, 1, 4, 2)
    vT = jnp.concatenate(
        [vT, jnp.ones((batch, DIFF_HEADS, nb, 1, tk), _bf16),
         jnp.zeros((batch, DIFF_HEADS, nb, BF16_SUBLANES - 1, tk), _bf16)], axis=3)
    return (qT.reshape(units, 2, QK_WIDTH, seq), k.reshape(units, nb, tk, QK_WIDTH),
            vT.reshape(units, nb, DIFF_V_ROWS, tk))


def kernel(x, ffn1_norm, ffn1_w_gate_up, ffn1_w_down, mix_norm, w_in, forget_bias,
           lambda_q1, lambda_k1, lambda_q2, lambda_k2, diff_subln, w_out,
           ffn2_norm, ffn2_w_gate_up, ffn2_w_down, final_norm):
    batch, seq, _ = x.shape
    tk = min(ATTN_BLOCK, seq)
    cos2, sin2 = _rope_tables(seq)
    for layer in range(DEPTH):
        lambda_init = 0.8 - 0.6 * math.exp(-0.3 * layer)
        x2d = _ffn(x.reshape(batch * seq, D_MODEL), ffn1_norm[layer],
                   ffn1_w_gate_up[layer].astype(_bf16), ffn1_w_down[layer].astype(_bf16),
                   final_norm, False)
        x = x2d.reshape(batch, seq, D_MODEL)

        w_all, w_ff = _prep_w_in(w_in[layer])
        fbias = jnp.pad(forget_bias[layer], (0, LANES - FOX_HEADS)).reshape(1, LANES)
        p, c_hi, c_mid, c_lo = _inproj(x, mix_norm[layer], w_all, w_ff, fbias, cos2, sin2)

        oT_fox = _attention(*_fox_layouts(p, (c_hi, c_mid, c_lo), tk), (),
                            n_comp=1, dv=FOX_HEAD_DIM, lambda_init=lambda_init)
        lam_vecs = tuple(v[layer].reshape(1, DIFF_HEAD_DIM)
                         for v in (lambda_q1, lambda_k1, lambda_q2, lambda_k2))
        oT_diff = _attention(*_diff_layouts(p, tk),
                             lam_vecs + (diff_subln[layer].reshape(DIFF_V_DIM, 1),),
                             n_comp=2, dv=DIFF_V_DIM, lambda_init=lambda_init)
        x = _outproj(x, oT_fox.reshape(batch, FOX_WIDTH, seq),
                     oT_diff.reshape(batch, DIFF_WIDTH, seq), w_out[layer].astype(_bf16))

        x2d = _ffn(x.reshape(batch * seq, D_MODEL), ffn2_norm[layer],
                   ffn2_w_gate_up[layer].astype(_bf16), ffn2_w_down[layer].astype(_bf16),
                   final_norm, layer == DEPTH - 1)
        x = x2d.reshape(batch, seq, D_MODEL)
    return x
```

```python
import functools
import math

import numpy as np
import jax
import jax.numpy as jnp
from jax import lax
from jax.experimental import pallas as pl
from jax.experimental.pallas import tpu as pltpu

D_MODEL = 1024
DEPTH = 2
FOX_HEADS = 8
FOX_HEAD_DIM = 64
DIFF_HEADS = 4
DIFF_HEAD_DIM = 64
DIFF_V_DIM = 2 * DIFF_HEAD_DIM
FOX_WIDTH = FOX_HEADS * FOX_HEAD_DIM
DIFF_WIDTH = DIFF_HEADS * DIFF_V_DIM
D_FF = 2816
ROPE_THETA = 10000.0
NORM_EPS = 1e-5
FFN_RES_WEIGHT = 0.5

LOG2E = 1.4426950408889634
Q_SCALE = FOX_HEAD_DIM ** -0.5 * LOG2E
MASKED_LOGIT = -1e30

LANES = 128
BF16_SUBLANES = 16
VMEM_LIMIT_BYTES = 56 * 1024 * 1024

FFN_TOKENS = 512
FF_CHUNK = D_FF
PROJ_TOKENS = 512
ATTN_BLOCK = 1024

QK_WIDTH = LANES
FOX_V_ROWS = FOX_HEAD_DIM + BF16_SUBLANES
DIFF_V_ROWS = DIFF_V_DIM + BF16_SUBLANES
GROUP = 512
GATE_ROWS = 16
N_SPLIT = 3
AUX0 = FOX_HEAD_DIM

_f32 = jnp.float32
_bf16 = jnp.bfloat16
_NT = (((1,), (1,)), ((), ()))
_TN = (((0,), (0,)), ((), ()))


def _rmsnorm(x, gain):
    return x * lax.rsqrt(jnp.mean(x * x, axis=-1, keepdims=True) + NORM_EPS) * gain


def _mm(a, b):
    return jnp.dot(a, b, preferred_element_type=_f32)


def _split3(v):
    hi = v.astype(_bf16).astype(_f32)
    r = v - hi
    mid = r.astype(_bf16).astype(_f32)
    lo = (r - mid).astype(_bf16).astype(_f32)
    return hi, mid, lo


def _ffn_kernel(x_ref, gain_ref, wg_ref, wu_ref, wd_ref, fgain_ref, o_ref, *, final_norm):
    x = x_ref[...]
    h = _rmsnorm(x, gain_ref[...]).astype(_bf16)
    y = jnp.zeros_like(x)
    for c in range(D_FF // FF_CHUNK):
        cols = slice(c * FF_CHUNK, (c + 1) * FF_CHUNK)
        g = _mm(h, wg_ref[:, cols])
        u = _mm(h, wu_ref[:, cols])
        a = (g * jax.nn.sigmoid(g) * u).astype(_bf16)
        y = y + _mm(a, wd_ref[cols, :])
    out = x + FFN_RES_WEIGHT * y
    if final_norm:
        out = _rmsnorm(out, fgain_ref[...])
    o_ref[...] = out


def _ffn(x2d, gain, w_gate_up, w_down, final_gain, final_norm):
    tokens = x2d.shape[0]
    resident = dict(pipeline_mode=pl.Buffered(1))
    return pl.pallas_call(
        functools.partial(_ffn_kernel, final_norm=final_norm),
        grid=(tokens // FFN_TOKENS,),
        in_specs=[
            pl.BlockSpec((FFN_TOKENS, D_MODEL), lambda i: (i, 0)),
            pl.BlockSpec((1, D_MODEL), lambda i: (0, 0)),
            pl.BlockSpec((D_MODEL, D_FF), lambda i: (0, 0), **resident),
            pl.BlockSpec((D_MODEL, D_FF), lambda i: (0, 1), **resident),
            pl.BlockSpec((D_FF, D_MODEL), lambda i: (0, 0), **resident),
            pl.BlockSpec((1, D_MODEL), lambda i: (0, 0)),
        ],
        out_specs=pl.BlockSpec((FFN_TOKENS, D_MODEL), lambda i: (i, 0)),
        out_shape=jax.ShapeDtypeStruct((tokens, D_MODEL), _f32),
        compiler_params=pltpu.CompilerParams(
            dimension_semantics=("arbitrary",), vmem_limit_bytes=VMEM_LIMIT_BYTES),
        name="ffn",
    )(x2d, gain.reshape(1, D_MODEL), w_gate_up, w_gate_up, w_down,
      final_gain.reshape(1, D_MODEL))


def _inproj_kernel(x_ref, gain_ref, wrows_ref, wcols_ref, fbias_ref, place_ref,
                   cosT_ref, sinT_ref, cos_ref, sin_ref,
                   fq_ref, fk_ref, fv_ref, dq_ref, dk_ref, dv_ref, carry_ref):
    tm = x_ref.shape[0]
    h = _rmsnorm(x_ref[...], gain_ref[...]).astype(_bf16)
    projT = lax.dot_general(wrows_ref[...], h, _NT, preferred_element_type=_f32)
    proj = _mm(h, wcols_ref[...])
    fqT, dqT, dq_rotT, fvT, dvT = [projT[n * GROUP:(n + 1) * GROUP] for n in range(5)]
    gateT = projT[5 * GROUP:5 * GROUP + GATE_ROWS] + fbias_ref[...]

    log_fT = jnp.minimum(gateT, 0.0) - jnp.log1p(jnp.exp(-jnp.abs(gateT)))
    row = lax.broadcasted_iota(jnp.int32, (tm, tm + LANES), 0)
    col = lax.broadcasted_iota(jnp.int32, (tm, tm + LANES), 1)
    upper = jnp.where(row <= col, 1.0, 0.0).astype(_bf16)
    pieces = jnp.concatenate([p.astype(_bf16) for p in _split3(log_fT)], axis=0)
    sums = _mm(pieces, upper)
    sums = sum(sums[n * GATE_ROWS:(n + 1) * GATE_ROWS] for n in range(N_SPLIT))

    @pl.when(pl.program_id(1) == 0)
    def _():
        carry_ref[...] = jnp.zeros_like(carry_ref)

    carry = carry_ref[...]
    cT = sums[:, :tm] + jnp.concatenate([carry] * (tm // LANES), axis=1)
    carry_ref[...] = carry + sums[:, tm:]
    c_parts = _split3(cT * LOG2E)

    sub = lax.broadcasted_iota(jnp.int32, (BF16_SUBLANES, tm), 0)
    ones_row = jnp.where(sub == 0, 1.0, 0.0).astype(_bf16)
    for hd in range(FOX_HEADS):
        rows = slice(hd * FOX_HEAD_DIM, (hd + 1) * FOX_HEAD_DIM)
        fq_ref[hd, 0:AUX0, :] = (fqT[rows] * Q_SCALE).astype(_bf16)
        aux = jnp.where(sub < 2 * N_SPLIT, -1.0, 0.0)
        for n in reversed(range(N_SPLIT)):
            aux = jnp.where(sub == n, c_parts[n][hd:hd + 1], aux)
        fq_ref[hd, AUX0:AUX0 + BF16_SUBLANES, :] = aux.astype(_bf16)
        fq_ref[hd, AUX0 + BF16_SUBLANES:, :] = jnp.zeros(
            (QK_WIDTH - AUX0 - BF16_SUBLANES, tm), _bf16)
        fv_ref[hd, 0:FOX_HEAD_DIM, :] = fvT[rows].astype(_bf16)
        fv_ref[hd, FOX_HEAD_DIM:, :] = ones_row

    cosT = cosT_ref[...]
    sinT = sinT_ref[...]
    zeros_half = jnp.zeros((DIFF_HEAD_DIM, tm), _bf16)
    for hd in range(DIFF_HEADS):
        for comp in range(2):
            rows = slice((2 * hd + comp) * DIFF_HEAD_DIM, (2 * hd + comp + 1) * DIFF_HEAD_DIM)
            q = ((dqT[rows] * cosT + dq_rotT[rows] * sinT) * Q_SCALE).astype(_bf16)
            dq_ref[hd, comp, comp * DIFF_HEAD_DIM:(comp + 1) * DIFF_HEAD_DIM, :] = q
            dq_ref[hd, comp, (1 - comp) * DIFF_HEAD_DIM:(2 - comp) * DIFF_HEAD_DIM, :] = zeros_half
        dv_ref[hd, 0:DIFF_V_DIM, :] = dvT[hd * DIFF_V_DIM:(hd + 1) * DIFF_V_DIM].astype(_bf16)
        dv_ref[hd, DIFF_V_DIM:, :] = ones_row

    stacked = jnp.concatenate([p[0:FOX_HEADS] for p in c_parts]
                              + [jnp.zeros((FOX_HEADS, tm), _f32)], axis=0)
    placed = lax.dot_general(stacked, place_ref[...], _TN, preferred_element_type=_f32)
    lane = lax.broadcasted_iota(jnp.int32, (1, LANES), 1)
    ones_lanes = jnp.where((lane >= AUX0) & (lane < AUX0 + N_SPLIT), 1.0, 0.0)
    for hd in range(FOX_HEADS):
        cols = slice(hd * LANES, (hd + 1) * LANES)
        fk_ref[hd] = (proj[:, cols] + placed[:, cols] + ones_lanes).astype(_bf16)

    cos = jnp.concatenate([cos_ref[...]] * (GROUP // LANES), axis=1)
    sin = jnp.concatenate([sin_ref[...]] * (GROUP // LANES), axis=1)
    k0 = FOX_HEADS * LANES
    dk = proj[:, k0:k0 + GROUP] * cos + proj[:, k0 + GROUP:k0 + 2 * GROUP] * sin
    for hd in range(DIFF_HEADS):
        dk_ref[hd] = dk[:, hd * LANES:(hd + 1) * LANES].astype(_bf16)


def _inproj(x3d, gain, w_rows, w_cols, fbias, place, tables, tk):
    batch, seq, _ = x3d.shape
    tm = min(PROJ_TOKENS, seq)
    per_blk = tk // tm
    nb = seq // tk
    resident = dict(pipeline_mode=pl.Buffered(1))
    cosT, sinT, cos2, sin2 = tables
    return pl.pallas_call(
        _inproj_kernel,
        grid=(batch, seq // tm),
        in_specs=[
            pl.BlockSpec((None, tm, D_MODEL), lambda b, i: (b, i, 0)),
            pl.BlockSpec((1, D_MODEL), lambda b, i: (0, 0)),
            pl.BlockSpec(w_rows.shape, lambda b, i: (0, 0), **resident),
            pl.BlockSpec(w_cols.shape, lambda b, i: (0, 0), **resident),
            pl.BlockSpec((GATE_ROWS, 1), lambda b, i: (0, 0)),
            pl.BlockSpec(place.shape, lambda b, i: (0, 0)),
            pl.BlockSpec((DIFF_HEAD_DIM, tm), lambda b, i: (0, i)),
            pl.BlockSpec((DIFF_HEAD_DIM, tm), lambda b, i: (0, i)),
            pl.BlockSpec((tm, LANES), lambda b, i: (i, 0)),
            pl.BlockSpec((tm, LANES), lambda b, i: (i, 0)),
        ],
        out_specs=[
            pl.BlockSpec((None, FOX_HEADS, QK_WIDTH, tm), lambda b, i: (b, 0, 0, i)),
            pl.BlockSpec((None, FOX_HEADS, tm, QK_WIDTH), lambda b, i: (b, 0, i, 0)),
            pl.BlockSpec((None, FOX_HEADS, None, FOX_V_ROWS, tm),
                         lambda b, i: (b, 0, i // per_blk, 0, i % per_blk)),
            pl.BlockSpec((None, DIFF_HEADS, 2, QK_WIDTH, tm), lambda b, i: (b, 0, 0, 0, i)),
            pl.BlockSpec((None, DIFF_HEADS, tm, QK_WIDTH), lambda b, i: (b, 0, i, 0)),
            pl.BlockSpec((None, DIFF_HEADS, None, DIFF_V_ROWS, tm),
                         lambda b, i: (b, 0, i // per_blk, 0, i % per_blk)),
        ],
        out_shape=[
            jax.ShapeDtypeStruct((batch, FOX_HEADS, QK_WIDTH, seq), _bf16),
            jax.ShapeDtypeStruct((batch, FOX_HEADS, seq, QK_WIDTH), _bf16),
            jax.ShapeDtypeStruct((batch, FOX_HEADS, nb, FOX_V_ROWS, tk), _bf16),
            jax.ShapeDtypeStruct((batch, DIFF_HEADS, 2, QK_WIDTH, seq), _bf16),
            jax.ShapeDtypeStruct((batch, DIFF_HEADS, seq, QK_WIDTH), _bf16),
            jax.ShapeDtypeStruct((batch, DIFF_HEADS, nb, DIFF_V_ROWS, tk), _bf16),
        ],
        scratch_shapes=[pltpu.VMEM((GATE_ROWS, LANES), _f32)],
        compiler_params=pltpu.CompilerParams(
            dimension_semantics=("arbitrary", "arbitrary"),
            vmem_limit_bytes=VMEM_LIMIT_BYTES),
        name="inproj",
    )(x3d, gain.reshape(1, D_MODEL), w_rows, w_cols, fbias, place, cosT, sinT, cos2, sin2)


def _attn_kernel(*refs, n_comp, dv, lambda_init):
    if n_comp == 1:
        qT_ref, k_ref, vT_ref, o_ref = refs[:4]
    else:
        (qT_ref, k_ref, vT_ref, lq1_ref, lk1_ref, lq2_ref, lk2_ref, subln_ref,
         o_ref) = refs[:9]
    m_ref, acc_ref, s_even, s_odd, mb_even, mb_odd = refs[-6:]
    tk = k_ref.shape[1]
    tq = qT_ref.shape[2]
    qi = pl.program_id(1)
    m_ref[...] = jnp.full(m_ref.shape, MASKED_LOGIT, _f32)
    acc_ref[...] = jnp.zeros(acc_ref.shape, _f32)

    def produce(j, s_ref, mb_ref):
        k_blk = k_ref[j]
        for c in range(n_comp):
            s = _mm(k_blk, qT_ref[c])
            s_ref[c] = s
            mb_ref[c] = jnp.max(s, axis=0, keepdims=True)

    def consume(j, s_ref, mb_ref, diagonal):
        v_blk = vT_ref[j]
        for c in range(n_comp):
            s = s_ref[c]
            if diagonal:
                key = lax.broadcasted_iota(jnp.int32, (tk, tq), 0)
                qry = lax.broadcasted_iota(jnp.int32, (tk, tq), 1)
                s = jnp.where(key <= qry, s, MASKED_LOGIT)
                m_blk = jnp.max(s, axis=0, keepdims=True)
            else:
                m_blk = mb_ref[c]
            m_old = m_ref[c]
            m_new = jnp.maximum(m_old, m_blk)
            alpha = jnp.exp2(m_old - m_new)
            p = jnp.exp2(s - m_new).astype(_bf16)
            acc_ref[c] = alpha * acc_ref[c] + _mm(v_blk, p)
            m_ref[c] = m_new

    produce(0, s_even, mb_even)

    def pair(i, carry):
        j = 2 * i
        produce(j + 1, s_odd, mb_odd)
        consume(j, s_even, mb_even, False)
        produce(j + 2, s_even, mb_even)
        consume(j + 1, s_odd, mb_odd, False)
        return carry

    lax.fori_loop(0, qi // 2, pair, 0)

    @pl.when(qi % 2 == 1)
    def _():
        produce(qi, s_odd, mb_odd)
        consume(qi - 1, s_even, mb_even, False)
        consume(qi, s_odd, mb_odd, True)

    @pl.when(qi % 2 == 0)
    def _():
        consume(qi, s_even, mb_even, True)

    if n_comp == 1:
        acc = acc_ref[0]
        o_ref[...] = (acc[:dv, :] / acc[dv:dv + 1, :]).astype(o_ref.dtype)
    else:
        a1 = acc_ref[0]
        a2 = acc_ref[1]
        lam = (jnp.exp(jnp.sum(lq1_ref[...] * lk1_ref[...], axis=1, keepdims=True))
               - jnp.exp(jnp.sum(lq2_ref[...] * lk2_ref[...], axis=1, keepdims=True))
               + lambda_init)
        o = a1[:dv, :] / a1[dv:dv + 1, :] - lam * (a2[:dv, :] / a2[dv:dv + 1, :])
        o = o * lax.rsqrt(jnp.mean(o * o, axis=0, keepdims=True) + NORM_EPS)
        o_ref[...] = (o * subln_ref[...] * (1.0 - lambda_init)).astype(o_ref.dtype)


def _attention(qT, k, vT, extras, *, n_comp, dv, lambda_init):
    units, _, _, seq = qT.shape
    n_blocks, tk = k.shape[1], k.shape[2]
    v_rows = vT.shape[2]
    tq = tk
    extra_specs = [pl.BlockSpec(e.shape, lambda u, i: (0, 0)) for e in extras]
    return pl.pallas_call(
        functools.partial(_attn_kernel, n_comp=n_comp, dv=dv, lambda_init=lambda_init),
        grid=(units, seq // tq),
        in_specs=[
            pl.BlockSpec((None, n_comp, QK_WIDTH, tq), lambda u, i: (u, 0, 0, i)),
            pl.BlockSpec((None, n_blocks, tk, QK_WIDTH), lambda u, i: (u, 0, 0, 0)),
            pl.BlockSpec((None, n_blocks, v_rows, tk), lambda u, i: (u, 0, 0, 0)),
        ] + extra_specs,
        out_specs=pl.BlockSpec((None, dv, tq), lambda u, i: (u, 0, i)),
        out_shape=jax.ShapeDtypeStruct((units, dv, seq), _bf16),
        scratch_shapes=[
            pltpu.VMEM((n_comp, 1, tq), _f32),
            pltpu.VMEM((n_comp, v_rows, tq), _f32),
            pltpu.VMEM((n_comp, tk, tq), _f32),
            pltpu.VMEM((n_comp, tk, tq), _f32),
            pltpu.VMEM((n_comp, 1, tq), _f32),
            pltpu.VMEM((n_comp, 1, tq), _f32),
        ],
        compiler_params=pltpu.CompilerParams(
            dimension_semantics=("arbitrary", "arbitrary"),
            vmem_limit_bytes=VMEM_LIMIT_BYTES),
        name="fox_attn" if n_comp == 1 else "diff_attn",
    )(qT, k, vT, *extras)


def _outproj_kernel(x_ref, of_ref, od_ref, wf_ref, wd_ref, o_ref):
    y = lax.dot_general(of_ref[...], wf_ref[...], _TN, preferred_element_type=_f32)
    y = y + lax.dot_general(od_ref[...], wd_ref[...], _TN, preferred_element_type=_f32)
    o_ref[...] = x_ref[...] + y


def _outproj(x3d, oT_fox, oT_diff, w_out):
    batch, seq, _ = x3d.shape
    tm = min(PROJ_TOKENS, seq)
    resident = dict(pipeline_mode=pl.Buffered(1))
    return pl.pallas_call(
        _outproj_kernel,
        grid=(batch, seq // tm),
        in_specs=[
            pl.BlockSpec((None, tm, D_MODEL), lambda b, i: (b, i, 0)),
            pl.BlockSpec((None, FOX_WIDTH, tm), lambda b, i: (b, 0, i)),
            pl.BlockSpec((None, DIFF_WIDTH, tm), lambda b, i: (b, 0, i)),
            pl.BlockSpec((FOX_WIDTH, D_MODEL), lambda b, i: (0, 0), **resident),
            pl.BlockSpec((DIFF_WIDTH, D_MODEL), lambda b, i: (1, 0), **resident),
        ],
        out_specs=pl.BlockSpec((None, tm, D_MODEL), lambda b, i: (b, i, 0)),
        out_shape=jax.ShapeDtypeStruct((batch, seq, D_MODEL), _f32),
        compiler_params=pltpu.CompilerParams(
            dimension_semantics=("arbitrary", "arbitrary"),
            vmem_limit_bytes=VMEM_LIMIT_BYTES),
        name="outproj",
    )(x3d, oT_fox, oT_diff, w_out, w_out)


def _rotate_half_columns(w):
    g = w.reshape(w.shape[0], -1, DIFF_HEAD_DIM)
    half = DIFF_HEAD_DIM // 2
    return jnp.concatenate([-g[..., half:], g[..., :half]], axis=-1).reshape(w.shape)


def _prep_w_in(w_in):
    o = 0
    parts = {}
    for name, width in (("fq", FOX_WIDTH), ("fk", FOX_WIDTH), ("fv", FOX_WIDTH),
                        ("ff", FOX_HEADS), ("dq", GROUP), ("dk", GROUP), ("dv", DIFF_WIDTH)):
        parts[name] = w_in[:, o:o + width]
        o += width
    w_rows = jnp.concatenate(
        [parts["fq"], parts["dq"], _rotate_half_columns(parts["dq"]), parts["fv"], parts["dv"],
         jnp.pad(parts["ff"], ((0, 0), (0, GATE_ROWS - FOX_HEADS)))], axis=1).T
    fk_pad = jnp.pad(parts["fk"].reshape(D_MODEL, FOX_HEADS, FOX_HEAD_DIM),
                     ((0, 0), (0, 0), (0, LANES - FOX_HEAD_DIM))).reshape(D_MODEL, FOX_HEADS * LANES)
    w_cols = jnp.concatenate([fk_pad, parts["dk"], _rotate_half_columns(parts["dk"])], axis=1)
    return w_rows.astype(_bf16), w_cols.astype(_bf16)


def _placement_matrix():
    place = np.zeros(((N_SPLIT + 1) * FOX_HEADS, FOX_HEADS * LANES), np.float32)
    for n in range(N_SPLIT):
        for hd in range(FOX_HEADS):
            place[n * FOX_HEADS + hd, hd * LANES + AUX0 + N_SPLIT + n] = 1.0
    return jnp.asarray(place)


def _rope_tables(seq):
    inv_freq = 1.0 / (ROPE_THETA ** (jnp.arange(0, DIFF_HEAD_DIM, 2, dtype=_f32) / DIFF_HEAD_DIM))
    freqs = jnp.arange(seq, dtype=_f32)[:, None] * inv_freq[None, :]
    emb = jnp.concatenate([freqs, freqs], axis=-1)
    cos, sin = jnp.cos(emb), jnp.sin(emb)
    return (cos.T, sin.T, jnp.concatenate([cos, cos], axis=-1), jnp.concatenate([sin, sin], axis=-1))


def kernel(x, ffn1_norm, ffn1_w_gate_up, ffn1_w_down, mix_norm, w_in, forget_bias,
           lambda_q1, lambda_k1, lambda_q2, lambda_k2, diff_subln, w_out,
           ffn2_norm, ffn2_w_gate_up, ffn2_w_down, final_norm):
    batch, seq, _ = x.shape
    tk = min(ATTN_BLOCK, seq)
    nb = seq // tk
    tables = _rope_tables(seq)
    place = _placement_matrix()
    for layer in range(DEPTH):
        lambda_init = 0.8 - 0.6 * math.exp(-0.3 * layer)
        x2d = _ffn(x.reshape(batch * seq, D_MODEL), ffn1_norm[layer],
                   ffn1_w_gate_up[layer].astype(_bf16), ffn1_w_down[layer].astype(_bf16),
                   final_norm, False)
        x = x2d.reshape(batch, seq, D_MODEL)

        w_rows, w_cols = _prep_w_in(w_in[layer])
        fbias = jnp.pad(forget_bias[layer], (0, GATE_ROWS - FOX_HEADS)).reshape(GATE_ROWS, 1)
        fq, fk, fv, dq, dk, dv = _inproj(x, mix_norm[layer], w_rows, w_cols, fbias, place,
                                         tables, tk)

        fox_units = batch * FOX_HEADS
        oT_fox = _attention(fq.reshape(fox_units, 1, QK_WIDTH, seq),
                            fk.reshape(fox_units, nb, tk, QK_WIDTH),
                            fv.reshape(fox_units, nb, FOX_V_ROWS, tk), (),
                            n_comp=1, dv=FOX_HEAD_DIM, lambda_init=lambda_init)
        diff_units = batch * DIFF_HEADS
        lam_vecs = tuple(v[layer].reshape(1, DIFF_HEAD_DIM)
                         for v in (lambda_q1, lambda_k1, lambda_q2, lambda_k2))
        oT_diff = _attention(dq.reshape(diff_units, 2, QK_WIDTH, seq),
                             dk.reshape(diff_units, nb, tk, QK_WIDTH),
                             dv.reshape(diff_units, nb, DIFF_V_ROWS, tk),
                             lam_vecs + (diff_subln[layer].reshape(DIFF_V_DIM, 1),),
                             n_comp=2, dv=DIFF_V_DIM, lambda_init=lambda_init)
        x = _outproj(x, oT_fox.reshape(batch, FOX_WIDTH, seq),
                     oT_diff.reshape(batch, DIFF_WIDTH, seq), w_out[layer].astype(_bf16))

        x2d = _ffn(x.reshape(batch * seq, D_MODEL), ffn2_norm[layer],
                   ffn2_w_gate_up[layer].astype(_bf16), ffn2_w_down[layer].astype(_bf16),
                   final_norm, layer == DEPTH - 1)
        x = x2d.reshape(batch, seq, D_MODEL)
    return x
```

```python
import functools
import math

import numpy as np
import jax
import jax.numpy as jnp
from jax import lax
from jax.experimental import pallas as pl
from jax.experimental.pallas import tpu as pltpu

D_MODEL = 1024
DEPTH = 2
FOX_HEADS = 8
FOX_HEAD_DIM = 64
DIFF_HEADS = 4
DIFF_HEAD_DIM = 64
DIFF_V_DIM = 2 * DIFF_HEAD_DIM
FOX_WIDTH = FOX_HEADS * FOX_HEAD_DIM
DIFF_WIDTH = DIFF_HEADS * DIFF_V_DIM
D_FF = 2816
ROPE_THETA = 10000.0
NORM_EPS = 1e-5
FFN_RES_WEIGHT = 0.5

LOG2E = 1.4426950408889634
Q_SCALE = FOX_HEAD_DIM ** -0.5 * LOG2E
MASKED_LOGIT = -1e30

LANES = 128
BF16_SUBLANES = 16
VMEM_LIMIT_BYTES = 56 * 1024 * 1024

FFN_TOKENS = 512
FF_CHUNK = D_FF
PROJ_TOKENS = 512
ATTN_BLOCK = 1024

QK_WIDTH = LANES
FOX_V_ROWS = FOX_HEAD_DIM + BF16_SUBLANES
DIFF_V_ROWS = DIFF_V_DIM + BF16_SUBLANES
GROUP = 512
GATE_ROWS = 16
N_SPLIT = 3
AUX0 = FOX_HEAD_DIM

_f32 = jnp.float32
_bf16 = jnp.bfloat16
_NT = (((1,), (1,)), ((), ()))
_TN = (((0,), (0,)), ((), ()))


def _rmsnorm(x, gain):
    return x * lax.rsqrt(jnp.mean(x * x, axis=-1, keepdims=True) + NORM_EPS) * gain


def _mm(a, b):
    return jnp.dot(a, b, preferred_element_type=_f32)


def _split3(v):
    hi = v.astype(_bf16).astype(_f32)
    r = v - hi
    mid = r.astype(_bf16).astype(_f32)
    lo = (r - mid).astype(_bf16).astype(_f32)
    return hi, mid, lo


def _ffn_kernel(*refs, with_mix, final_norm):
    if with_mix:
        of_ref, od_ref, wof_ref, wod_ref = refs[:4]
        refs = refs[4:]
    x_ref, gain_ref, wg_ref, wu_ref, wd_ref, fgain_ref, o_ref = refs
    x = x_ref[...]
    if with_mix:
        x = x + lax.dot_general(of_ref[...], wof_ref[...], _TN, preferred_element_type=_f32)
        x = x + lax.dot_general(od_ref[...], wod_ref[...], _TN, preferred_element_type=_f32)
    h = _rmsnorm(x, gain_ref[...]).astype(_bf16)
    y = jnp.zeros_like(x)
    for c in range(D_FF // FF_CHUNK):
        cols = slice(c * FF_CHUNK, (c + 1) * FF_CHUNK)
        g = _mm(h, wg_ref[:, cols])
        u = _mm(h, wu_ref[:, cols])
        a = (g * jax.nn.sigmoid(g) * u).astype(_bf16)
        y = y + _mm(a, wd_ref[cols, :])
    out = x + FFN_RES_WEIGHT * y
    if final_norm:
        out = _rmsnorm(out, fgain_ref[...])
    o_ref[...] = out


def _ffn(x3d, mix, gain, w_gate_up, w_down, final_gain, final_norm):
    batch, seq, _ = x3d.shape
    tm = min(FFN_TOKENS, seq)
    resident = dict(pipeline_mode=pl.Buffered(1))
    mix_specs, mix_args = [], []
    if mix is not None:
        oT_fox, oT_diff, w_out = mix
        mix_specs = [
            pl.BlockSpec((None, FOX_WIDTH, tm), lambda b, i: (b, 0, i)),
            pl.BlockSpec((None, DIFF_WIDTH, tm), lambda b, i: (b, 0, i)),
            pl.BlockSpec((FOX_WIDTH, D_MODEL), lambda b, i: (0, 0), **resident),
            pl.BlockSpec((DIFF_WIDTH, D_MODEL), lambda b, i: (1, 0), **resident),
        ]
        mix_args = [oT_fox, oT_diff, w_out, w_out]
    return pl.pallas_call(
        functools.partial(_ffn_kernel, with_mix=mix is not None, final_norm=final_norm),
        grid=(batch, seq // tm),
        in_specs=mix_specs + [
            pl.BlockSpec((None, tm, D_MODEL), lambda b, i: (b, i, 0)),
            pl.BlockSpec((1, D_MODEL), lambda b, i: (0, 0)),
            pl.BlockSpec((D_MODEL, D_FF), lambda b, i: (0, 0), **resident),
            pl.BlockSpec((D_MODEL, D_FF), lambda b, i: (0, 1), **resident),
            pl.BlockSpec((D_FF, D_MODEL), lambda b, i: (0, 0), **resident),
            pl.BlockSpec((1, D_MODEL), lambda b, i: (0, 0)),
        ],
        out_specs=pl.BlockSpec((None, tm, D_MODEL), lambda b, i: (b, i, 0)),
        out_shape=jax.ShapeDtypeStruct((batch, seq, D_MODEL), _f32),
        compiler_params=pltpu.CompilerParams(
            dimension_semantics=("arbitrary", "arbitrary"),
            vmem_limit_bytes=VMEM_LIMIT_BYTES),
        name="mix_ffn" if mix is not None else "ffn",
    )(*mix_args, x3d, gain.reshape(1, D_MODEL), w_gate_up, w_gate_up, w_down,
      final_gain.reshape(1, D_MODEL))


def _inproj_kernel(x_ref, gain_ref, wrows_ref, wcols_ref, fbias_ref, place_ref,
                   cosT_ref, sinT_ref, cos_ref, sin_ref,
                   fq_ref, fk_ref, fv_ref, dq_ref, dk_ref, dv_ref, carry_ref):
    tm = x_ref.shape[0]
    h = _rmsnorm(x_ref[...], gain_ref[...]).astype(_bf16)
    projT = lax.dot_general(wrows_ref[...], h, _NT, preferred_element_type=_f32)
    proj = _mm(h, wcols_ref[...])
    fqT, dqT, dq_rotT, fvT, dvT = [projT[n * GROUP:(n + 1) * GROUP] for n in range(5)]
    gateT = projT[5 * GROUP:5 * GROUP + GATE_ROWS] + fbias_ref[...]

    log_fT = jnp.minimum(gateT, 0.0) - jnp.log1p(jnp.exp(-jnp.abs(gateT)))
    row = lax.broadcasted_iota(jnp.int32, (tm, tm + LANES), 0)
    col = lax.broadcasted_iota(jnp.int32, (tm, tm + LANES), 1)
    upper = jnp.where(row <= col, 1.0, 0.0).astype(_bf16)
    pieces = jnp.concatenate([p.astype(_bf16) for p in _split3(log_fT)], axis=0)
    sums = _mm(pieces, upper)
    sums = sum(sums[n * GATE_ROWS:(n + 1) * GATE_ROWS] for n in range(N_SPLIT))

    @pl.when(pl.program_id(1) == 0)
    def _():
        carry_ref[...] = jnp.zeros_like(carry_ref)

    carry = carry_ref[...]
    cT = sums[:, :tm] + jnp.concatenate([carry] * (tm // LANES), axis=1)
    carry_ref[...] = carry + sums[:, tm:]
    c_parts = _split3(cT * LOG2E)

    sub = lax.broadcasted_iota(jnp.int32, (BF16_SUBLANES, tm), 0)
    ones_row = jnp.where(sub == 0, 1.0, 0.0).astype(_bf16)
    for hd in range(FOX_HEADS):
        rows = slice(hd * FOX_HEAD_DIM, (hd + 1) * FOX_HEAD_DIM)
        fq_ref[hd, 0:AUX0, :] = (fqT[rows] * Q_SCALE).astype(_bf16)
        aux = jnp.where(sub < 2 * N_SPLIT, -1.0, 0.0)
        for n in reversed(range(N_SPLIT)):
            aux = jnp.where(sub == n, c_parts[n][hd:hd + 1], aux)
        fq_ref[hd, AUX0:AUX0 + BF16_SUBLANES, :] = aux.astype(_bf16)
        fq_ref[hd, AUX0 + BF16_SUBLANES:, :] = jnp.zeros(
            (QK_WIDTH - AUX0 - BF16_SUBLANES, tm), _bf16)
        fv_ref[hd, 0:FOX_HEAD_DIM, :] = fvT[rows].astype(_bf16)
        fv_ref[hd, FOX_HEAD_DIM:, :] = ones_row

    cosT = cosT_ref[...]
    sinT = sinT_ref[...]
    zeros_half = jnp.zeros((DIFF_HEAD_DIM, tm), _bf16)
    for hd in range(DIFF_HEADS):
        for comp in range(2):
            rows = slice((2 * hd + comp) * DIFF_HEAD_DIM, (2 * hd + comp + 1) * DIFF_HEAD_DIM)
            q = ((dqT[rows] * cosT + dq_rotT[rows] * sinT) * Q_SCALE).astype(_bf16)
            dq_ref[hd, comp, comp * DIFF_HEAD_DIM:(comp + 1) * DIFF_HEAD_DIM, :] = q
            dq_ref[hd, comp, (1 - comp) * DIFF_HEAD_DIM:(2 - comp) * DIFF_HEAD_DIM, :] = zeros_half
        dv_ref[hd, 0:DIFF_V_DIM, :] = dvT[hd * DIFF_V_DIM:(hd + 1) * DIFF_V_DIM].astype(_bf16)
        dv_ref[hd, DIFF_V_DIM:, :] = ones_row

    stacked = jnp.concatenate([p[0:FOX_HEADS] for p in c_parts]
                              + [jnp.zeros((FOX_HEADS, tm), _f32)], axis=0)
    placed = lax.dot_general(stacked, place_ref[...], _TN, preferred_element_type=_f32)
    lane = lax.broadcasted_iota(jnp.int32, (1, LANES), 1)
    ones_lanes = jnp.where((lane >= AUX0) & (lane < AUX0 + N_SPLIT), 1.0, 0.0)
    for hd in range(FOX_HEADS):
        cols = slice(hd * LANES, (hd + 1) * LANES)
        fk_ref[hd] = (proj[:, cols] + placed[:, cols] + ones_lanes).astype(_bf16)

    cos = jnp.concatenate([cos_ref[...]] * (GROUP // LANES), axis=1)
    sin = jnp.concatenate([sin_ref[...]] * (GROUP // LANES), axis=1)
    k0 = FOX_HEADS * LANES
    dk = proj[:, k0:k0 + GROUP] * cos + proj[:, k0 + GROUP:k0 + 2 * GROUP] * sin
    for hd in range(DIFF_HEADS):
        dk_ref[hd] = dk[:, hd * LANES:(hd + 1) * LANES].astype(_bf16)


def _inproj(x3d, gain, w_rows, w_cols, fbias, place, tables, tk):
    batch, seq, _ = x3d.shape
    tm = min(PROJ_TOKENS, seq)
    per_blk = tk // tm
    nb = seq // tk
    resident = dict(pipeline_mode=pl.Buffered(1))
    cosT, sinT, cos2, sin2 = tables
    return pl.pallas_call(
        _inproj_kernel,
        grid=(batch, seq // tm),
        in_specs=[
            pl.BlockSpec((None, tm, D_MODEL), lambda b, i: (b, i, 0)),
            pl.BlockSpec((1, D_MODEL), lambda b, i: (0, 0)),
            pl.BlockSpec(w_rows.shape, lambda b, i: (0, 0), **resident),
            pl.BlockSpec(w_cols.shape, lambda b, i: (0, 0), **resident),
            pl.BlockSpec((GATE_ROWS, 1), lambda b, i: (0, 0)),
            pl.BlockSpec(place.shape, lambda b, i: (0, 0)),
            pl.BlockSpec((DIFF_HEAD_DIM, tm), lambda b, i: (0, i)),
            pl.BlockSpec((DIFF_HEAD_DIM, tm), lambda b, i: (0, i)),
            pl.BlockSpec((tm, LANES), lambda b, i: (i, 0)),
            pl.BlockSpec((tm, LANES), lambda b, i: (i, 0)),
        ],
        out_specs=[
            pl.BlockSpec((None, FOX_HEADS, QK_WIDTH, tm), lambda b, i: (b, 0, 0, i)),
            pl.BlockSpec((None, FOX_HEADS, tm, QK_WIDTH), lambda b, i: (b, 0, i, 0)),
            pl.BlockSpec((None, FOX_HEADS, None, FOX_V_ROWS, tm),
                         lambda b, i: (b, 0, i // per_blk, 0, i % per_blk)),
            pl.BlockSpec((None, DIFF_HEADS, 2, QK_WIDTH, tm), lambda b, i: (b, 0, 0, 0, i)),
            pl.BlockSpec((None, DIFF_HEADS, tm, QK_WIDTH), lambda b, i: (b, 0, i, 0)),
            pl.BlockSpec((None, DIFF_HEADS, None, DIFF_V_ROWS, tm),
                         lambda b, i: (b, 0, i // per_blk, 0, i % per_blk)),
        ],
        out_shape=[
            jax.ShapeDtypeStruct((batch, FOX_HEADS, QK_WIDTH, seq), _bf16),
            jax.ShapeDtypeStruct((batch, FOX_HEADS, seq, QK_WIDTH), _bf16),
            jax.ShapeDtypeStruct((batch, FOX_HEADS, nb, FOX_V_ROWS, tk), _bf16),
            jax.ShapeDtypeStruct((batch, DIFF_HEADS, 2, QK_WIDTH, seq), _bf16),
            jax.ShapeDtypeStruct((batch, DIFF_HEADS, seq, QK_WIDTH), _bf16),
            jax.ShapeDtypeStruct((batch, DIFF_HEADS, nb, DIFF_V_ROWS, tk), _bf16),
        ],
        scratch_shapes=[pltpu.VMEM((GATE_ROWS, LANES), _f32)],
        compiler_params=pltpu.CompilerParams(
            dimension_semantics=("arbitrary", "arbitrary"),
            vmem_limit_bytes=VMEM_LIMIT_BYTES),
        name="inproj",
    )(x3d, gain.reshape(1, D_MODEL), w_rows, w_cols, fbias, place, cosT, sinT, cos2, sin2)


def _attn_kernel(*refs, n_comp, dv, lambda_init):
    if n_comp == 1:
        qT_ref, k_ref, vT_ref, o_ref = refs[:4]
    else:
        (qT_ref, k_ref, vT_ref, lq1_ref, lk1_ref, lq2_ref, lk2_ref, subln_ref,
         o_ref) = refs[:9]
    m_ref, acc_ref, s_even, s_odd, mb_even, mb_odd = refs[-6:]
    tk = k_ref.shape[1]
    tq = tk
    g = pl.program_id(1)
    odd_tile = slice(tq, 2 * tq)
    even_tile = slice(0, tq)

    def reset():
        m_ref[...] = jnp.full(m_ref.shape, MASKED_LOGIT, _f32)
        acc_ref[...] = jnp.zeros(acc_ref.shape, _f32)

    def produce(j, qs, s_ref, mb_ref):
        k_blk = k_ref[j]
        for c in range(n_comp):
            s = _mm(k_blk, qT_ref[c, :, qs])
            s_ref[c] = s
            mb_ref[c] = jnp.max(s, axis=0, keepdims=True)

    def consume(j, s_ref, mb_ref, diagonal):
        v_blk = vT_ref[j]
        for c in range(n_comp):
            s = s_ref[c]
            if diagonal:
                key = lax.broadcasted_iota(jnp.int32, (tk, tq), 0)
                qry = lax.broadcasted_iota(jnp.int32, (tk, tq), 1)
                s = jnp.where(key <= qry, s, MASKED_LOGIT)
                m_blk = jnp.max(s, axis=0, keepdims=True)
            else:
                m_blk = mb_ref[c]
            m_old = m_ref[c]
            m_new = jnp.maximum(m_old, m_blk)
            alpha = jnp.exp2(m_old - m_new)
            p = jnp.exp2(s - m_new).astype(_bf16)
            acc_ref[c] = alpha * acc_ref[c] + _mm(v_blk, p)
            m_ref[c] = m_new

    def pairs(qs):
        def pair(i, carry):
            j = 2 * i
            produce(j + 1, qs, s_odd, mb_odd)
            consume(j, s_even, mb_even, False)
            produce(j + 2, qs, s_even, mb_even)
            consume(j + 1, s_odd, mb_odd, False)
            return carry

        lax.fori_loop(0, g, pair, 0)

    def finish(qs):
        if n_comp == 1:
            acc = acc_ref[0]
            o_ref[:, qs] = (acc[:dv, :] / acc[dv:dv + 1, :]).astype(o_ref.dtype)
        else:
            a1 = acc_ref[0]
            a2 = acc_ref[1]
            lam = (jnp.exp(jnp.sum(lq1_ref[...] * lk1_ref[...], axis=1, keepdims=True))
                   - jnp.exp(jnp.sum(lq2_ref[...] * lk2_ref[...], axis=1, keepdims=True))
                   + lambda_init)
            o = a1[:dv, :] / a1[dv:dv + 1, :] - lam * (a2[:dv, :] / a2[dv:dv + 1, :])
            o = o * lax.rsqrt(jnp.mean(o * o, axis=0, keepdims=True) + NORM_EPS)
            o_ref[:, qs] = (o * subln_ref[...] * (1.0 - lambda_init)).astype(o_ref.dtype)

    reset()
    produce(0, odd_tile, s_even, mb_even)
    pairs(odd_tile)
    produce(2 * g + 1, odd_tile, s_odd, mb_odd)
    consume(2 * g, s_even, mb_even, False)
    produce(0, even_tile, s_even, mb_even)
    consume(2 * g + 1, s_odd, mb_odd, True)
    finish(odd_tile)

    reset()
    pairs(even_tile)
    consume(2 * g, s_even, mb_even, True)
    finish(even_tile)


def _attention(qT, k, vT, extras, *, n_comp, dv, lambda_init):
    units, _, _, seq = qT.shape
    n_blocks, tk = k.shape[1], k.shape[2]
    v_rows = vT.shape[2]
    tq = tk
    assert n_blocks % 2 == 0, "query tiles are processed in (odd, even) pairs"
    extra_specs = [pl.BlockSpec(e.shape, lambda u, g: (0, 0)) for e in extras]
    return pl.pallas_call(
        functools.partial(_attn_kernel, n_comp=n_comp, dv=dv, lambda_init=lambda_init),
        grid=(units, n_blocks // 2),
        in_specs=[
            pl.BlockSpec((None, n_comp, QK_WIDTH, 2 * tq), lambda u, g: (u, 0, 0, g)),
            pl.BlockSpec((None, n_blocks, tk, QK_WIDTH), lambda u, g: (u, 0, 0, 0)),
            pl.BlockSpec((None, n_blocks, v_rows, tk), lambda u, g: (u, 0, 0, 0)),
        ] + extra_specs,
        out_specs=pl.BlockSpec((None, dv, 2 * tq), lambda u, g: (u, 0, g)),
        out_shape=jax.ShapeDtypeStruct((units, dv, seq), _bf16),
        scratch_shapes=[
            pltpu.VMEM((n_comp, 1, tq), _f32),
            pltpu.VMEM((n_comp, v_rows, tq), _f32),
            pltpu.VMEM((n_comp, tk, tq), _f32),
            pltpu.VMEM((n_comp, tk, tq), _f32),
            pltpu.VMEM((n_comp, 1, tq), _f32),
            pltpu.VMEM((n_comp, 1, tq), _f32),
        ],
        compiler_params=pltpu.CompilerParams(
            dimension_semantics=("arbitrary", "arbitrary"),
            vmem_limit_bytes=VMEM_LIMIT_BYTES),
        name="fox_attn" if n_comp == 1 else "diff_attn",
    )(qT, k, vT, *extras)


def _rotate_half_columns(w):
    g = w.reshape(w.shape[0], -1, DIFF_HEAD_DIM)
    half = DIFF_HEAD_DIM // 2
    return jnp.concatenate([-g[..., half:], g[..., :half]], axis=-1).reshape(w.shape)


def _prep_w_in(w_in):
    o = 0
    parts = {}
    for name, width in (("fq", FOX_WIDTH), ("fk", FOX_WIDTH), ("fv", FOX_WIDTH),
                        ("ff", FOX_HEADS), ("dq", GROUP), ("dk", GROUP), ("dv", DIFF_WIDTH)):
        parts[name] = w_in[:, o:o + width]
        o += width
    w_rows = jnp.concatenate(
        [parts["fq"], parts["dq"], _rotate_half_columns(parts["dq"]), parts["fv"], parts["dv"],
         jnp.pad(parts["ff"], ((0, 0), (0, GATE_ROWS - FOX_HEADS)))], axis=1).T
    fk_pad = jnp.pad(parts["fk"].reshape(D_MODEL, FOX_HEADS, FOX_HEAD_DIM),
                     ((0, 0), (0, 0), (0, LANES - FOX_HEAD_DIM))).reshape(D_MODEL, FOX_HEADS * LANES)
    w_cols = jnp.concatenate([fk_pad, parts["dk"], _rotate_half_columns(parts["dk"])], axis=1)
    return w_rows.astype(_bf16), w_cols.astype(_bf16)


def _placement_matrix():
    place = np.zeros(((N_SPLIT + 1) * FOX_HEADS, FOX_HEADS * LANES), np.float32)
    for n in range(N_SPLIT):
        for hd in range(FOX_HEADS):
            place[n * FOX_HEADS + hd, hd * LANES + AUX0 + N_SPLIT + n] = 1.0
    return jnp.asarray(place)


def _rope_tables(seq):
    inv_freq = 1.0 / (ROPE_THETA ** (jnp.arange(0, DIFF_HEAD_DIM, 2, dtype=_f32) / DIFF_HEAD_DIM))
    freqs = jnp.arange(seq, dtype=_f32)[:, None] * inv_freq[None, :]
    emb = jnp.concatenate([freqs, freqs], axis=-1)
    cos, sin = jnp.cos(emb), jnp.sin(emb)
    return (cos.T, sin.T, jnp.concatenate([cos, cos], axis=-1), jnp.concatenate([sin, sin], axis=-1))


def kernel(x, ffn1_norm, ffn1_w_gate_up, ffn1_w_down, mix_norm, w_in, forget_bias,
           lambda_q1, lambda_k1, lambda_q2, lambda_k2, diff_subln, w_out,
           ffn2_norm, ffn2_w_gate_up, ffn2_w_down, final_norm):
    batch, seq, _ = x.shape
    tk = min(ATTN_BLOCK, seq)
    nb = seq // tk
    tables = _rope_tables(seq)
    place = _placement_matrix()
    for layer in range(DEPTH):
        lambda_init = 0.8 - 0.6 * math.exp(-0.3 * layer)
        x = _ffn(x, None, ffn1_norm[layer], ffn1_w_gate_up[layer].astype(_bf16),
                 ffn1_w_down[layer].astype(_bf16), final_norm, False)

        w_rows, w_cols = _prep_w_in(w_in[layer])
        fbias = jnp.pad(forget_bias[layer], (0, GATE_ROWS - FOX_HEADS)).reshape(GATE_ROWS, 1)
        fq, fk, fv, dq, dk, dv = _inproj(x, mix_norm[layer], w_rows, w_cols, fbias, place,
                                         tables, tk)

        fox_units = batch * FOX_HEADS
        oT_fox = _attention(fq.reshape(fox_units, 1, QK_WIDTH, seq),
                            fk.reshape(fox_units, nb, tk, QK_WIDTH),
                            fv.reshape(fox_units, nb, FOX_V_ROWS, tk), (),
                            n_comp=1, dv=FOX_HEAD_DIM, lambda_init=lambda_init)
        diff_units = batch * DIFF_HEADS
        lam_vecs = tuple(v[layer].reshape(1, DIFF_HEAD_DIM)
                         for v in (lambda_q1, lambda_k1, lambda_q2, lambda_k2))
        oT_diff = _attention(dq.reshape(diff_units, 2, QK_WIDTH, seq),
                             dk.reshape(diff_units, nb, tk, QK_WIDTH),
                             dv.reshape(diff_units, nb, DIFF_V_ROWS, tk),
                             lam_vecs + (diff_subln[layer].reshape(DIFF_V_DIM, 1),),
                             n_comp=2, dv=DIFF_V_DIM, lambda_init=lambda_init)
        mix = (oT_fox.reshape(batch, FOX_WIDTH, seq), oT_diff.reshape(batch, DIFF_WIDTH, seq),
               w_out[layer].astype(_bf16))
        x = _ffn(x, mix, ffn2_norm[layer], ffn2_w_gate_up[layer].astype(_bf16),
                 ffn2_w_down[layer].astype(_bf16), final_norm, layer == DEPTH - 1)
    return x
```
